```python
import jax, jax.numpy as jnp
from jax import lax
import numpy as np

D_MODEL = 2048
BATCH = 1
SEQ = 8192
DEPTH = 1

N_META = 16
MIX = D_MODEL
RET_WIDTH = MIX // 2
RET_HEADS = 4
RET_HEAD_DIM = RET_WIDTH // RET_HEADS
CONV_WIDTH = MIX - RET_WIDTH
CONV_GROUPS = 4
CONV_K = 31
CHUNK = 128
ROPE_BASE = 10000.0
EPS = 1e-6
IN_SPLITS = (RET_WIDTH, RET_WIDTH, RET_WIDTH, RET_WIDTH, CONV_WIDTH, CONV_WIDTH, CONV_WIDTH)
IN_WIDTH = sum(IN_SPLITS)

kernel_name = "hybrid_retention_conformer_block"


def rms_norm(x, g):
    xf = x.astype(jnp.float32)
    y = xf * lax.rsqrt(jnp.mean(xf * xf, axis=-1, keepdims=True) + EPS)
    return (y * g.astype(jnp.float32)).astype(x.dtype)


def rotary(x, pos):
    half = x.shape[-1] // 2
    inv_freq = ROPE_BASE ** (-jnp.arange(half, dtype=jnp.float32) / half)
    ang = pos[:, None] * inv_freq[None, :]
    cos = jnp.cos(ang)[None, :, None, :]
    sin = jnp.sin(ang)[None, :, None, :]
    x1, x2 = x[..., :half], x[..., half:]
    return jnp.concatenate([x1 * cos - x2 * sin, x1 * sin + x2 * cos], axis=-1)


def chunkwise_retention(q, k, v):
    b, l, h, d = q.shape
    n = l // CHUNK
    to_chunks = lambda t: t.reshape(b, n, CHUNK, h, t.shape[-1]).transpose(1, 0, 3, 2, 4)
    qc, kc, vc = to_chunks(q), to_chunks(k), to_chunks(v)
    gamma = 1.0 - jnp.exp2(-5.0 - jnp.arange(h, dtype=jnp.float32))
    log_g = jnp.log(gamma)
    idx = jnp.arange(CHUNK, dtype=jnp.float32)
    rel = idx[:, None] - idx[None, :]
    decay_mask = jnp.where(rel[None] >= 0,
                           jnp.exp(jnp.maximum(rel, 0.0)[None] * log_g[:, None, None]),
                           0.0)
    q_decay = jnp.exp((idx[None, :] + 1.0) * log_g[:, None])
    k_decay = jnp.exp((CHUNK - 1.0 - idx[None, :]) * log_g[:, None])
    chunk_decay = jnp.exp(CHUNK * log_g)

    def step(state, inp):
        qi, ki, vi = inp
        scores = jnp.einsum('bhqd,bhkd->bhqk', qi, ki) * decay_mask[None]
        inner = jnp.einsum('bhqk,bhkv->bhqv', scores, vi)
        cross = jnp.einsum('bhqd,bhdv->bhqv', qi * q_decay[None, :, :, None], state)
        new_state = state * chunk_decay[None, :, None, None] + jnp.einsum(
            'bhkd,bhkv->bhdv', ki * k_decay[None, :, :, None], vi)
        return new_state, inner + cross

    state0 = jnp.zeros((b, h, d, v.shape[-1]), jnp.float32)
    _, out = lax.scan(step, state0, (qc, kc, vc))
    return out.transpose(1, 0, 3, 2, 4).reshape(b, l, h, v.shape[-1])


def retention_group(q_in, k_in, v_in, g_in, gn_g):
    b, l, _ = q_in.shape
    pos = jnp.arange(l, dtype=jnp.float32)
    shp = (b, l, RET_HEADS, RET_HEAD_DIM)
    q = rotary(q_in.astype(jnp.float32).reshape(shp), pos)
    k = rotary(k_in.astype(jnp.float32).reshape(shp), pos) * (RET_HEAD_DIM ** -0.5)
    v = v_in.astype(jnp.float32).reshape(shp)
    lead = (-N_META) % CHUNK
    padw = ((0, 0), (lead, 0), (0, 0), (0, 0))
    y = chunkwise_retention(jnp.pad(q, padw), jnp.pad(k, padw), jnp.pad(v, padw))[:, lead:]
    mu = jnp.mean(y, axis=-1, keepdims=True)
    var = jnp.mean(jnp.square(y - mu), axis=-1, keepdims=True)
    y = ((y - mu) * lax.rsqrt(var + EPS)).reshape(b, l, RET_WIDTH) * gn_g.astype(jnp.float32)
    return (y * jax.nn.silu(g_in.astype(jnp.float32))).astype(q_in.dtype)


def conformer_conv_group(a_in, b_in, g_in, dw_w, dw_b, ln_g, ln_b, pw_w, pw_b):
    u = a_in * jax.nn.sigmoid(b_in)
    u = lax.conv_general_dilated(u, dw_w[:, None, :].astype(u.dtype), window_strides=(1,),
                                 padding=[(CONV_K - 1, 0)],
                                 dimension_numbers=('NWC', 'WIO', 'NWC'),
                                 feature_group_count=CONV_WIDTH) + dw_b
    uf = u.astype(jnp.float32)
    mu = jnp.mean(uf, axis=-1, keepdims=True)
    var = jnp.mean(jnp.square(uf - mu), axis=-1, keepdims=True)
    uf = (uf - mu) * lax.rsqrt(var + EPS) * ln_g.astype(jnp.float32) + ln_b.astype(jnp.float32)
    u = jax.nn.silu(uf).astype(a_in.dtype)
    u = jnp.einsum('blc,cd->bld', u, pw_w) + pw_b
    return u * jax.nn.silu(g_in)


def setup_inputs(seed: int = 0) -> dict:
    key = jax.random.key(seed)
    ks = jax.random.split(key, 14)
    f32 = jnp.float32
    nrm = lambda k, s, sc: jax.random.normal(k, s, f32) * sc
    return {
        "x": nrm(ks[0], (BATCH, SEQ, D_MODEL), 1.0),
        "meta_tokens": nrm(ks[1], (N_META, D_MODEL), 1.0),
        "ln_g": 1.0 + nrm(ks[2], (DEPTH, D_MODEL), 0.02),
        "w_in": nrm(ks[3], (DEPTH, D_MODEL, IN_WIDTH), D_MODEL ** -0.5),
        "ret_gn_g": 1.0 + nrm(ks[4], (DEPTH, RET_WIDTH), 0.02),
        "conv_dw_w": nrm(ks[5], (DEPTH, CONV_K, CONV_WIDTH), CONV_K ** -0.5),
        "conv_dw_b": nrm(ks[6], (DEPTH, CONV_WIDTH), 0.01),
        "conv_ln_g": 1.0 + nrm(ks[7], (DEPTH, CONV_WIDTH), 0.02),
        "conv_ln_b": nrm(ks[8], (DEPTH, CONV_WIDTH), 0.01),
        "conv_pw_w": nrm(ks[9], (DEPTH, CONV_WIDTH, CONV_WIDTH), CONV_WIDTH ** -0.5),
        "conv_pw_b": nrm(ks[10], (DEPTH, CONV_WIDTH), 0.01),
        "w_out": nrm(ks[11], (DEPTH, MIX, D_MODEL), MIX ** -0.5),
        "final_g": 1.0 + nrm(ks[12], (D_MODEL,), 0.02),
    }


def reference(x, meta_tokens, ln_g, w_in, ret_gn_g, conv_dw_w, conv_dw_b, conv_ln_g,
              conv_ln_b, conv_pw_w, conv_pw_b, w_out, final_g):
    b = x.shape[0]
    meta = jnp.broadcast_to(meta_tokens[None].astype(x.dtype), (b, N_META, D_MODEL))
    h = jnp.concatenate([meta, x], axis=1)
    offs = np.cumsum(IN_SPLITS)[:-1].tolist()
    for l in range(DEPTH):
        hn = rms_norm(h, ln_g[l])
        proj = jnp.einsum('bld,de->ble', hn, w_in[l])
        q_in, k_in, v_in, gr_in, a_in, b_in, gc_in = jnp.split(proj, offs, axis=-1)
        y_ret = retention_group(q_in, k_in, v_in, gr_in, ret_gn_g[l])
        y_conv = conformer_conv_group(a_in, b_in, gc_in, conv_dw_w[l], conv_dw_b[l],
                                      conv_ln_g[l], conv_ln_b[l], conv_pw_w[l], conv_pw_b[l])
        y = jnp.concatenate([y_ret, y_conv], axis=-1)
        h = h + jnp.einsum('ble,ed->bld', y, w_out[l])
    h = rms_norm(h, final_g)
    return h[:, N_META:]
```

```python
import functools

import numpy as np
import jax
import jax.numpy as jnp
from jax import lax
from jax.experimental import pallas as pl
from jax.experimental.pallas import tpu as pltpu

D_MODEL = 2048
SEQ = 8192
N_META = 16
RET_WIDTH = 1024
RET_HEADS = 4
HEAD_DIM = 256
HALF = HEAD_DIM // 2
CONV_WIDTH = 1024
CONV_K = 31
CHUNK = 128
ROPE_BASE = 10000.0
EPS = 1e-6
IN_WIDTH = 4 * RET_WIDTH + 3 * CONV_WIDTH

Q0, K0, V0, G0, A0, B0, C0 = 0, 1024, 2048, 3072, 4096, 5120, 6144

LANES = 128
TM = 256
N_TILES = SEQ // TM
HIST = 32
TM_OUT = 512
CONV_RB = 64
VMEM_LIMIT_MIX = 58 * 1024 * 1024
VMEM_LIMIT_OUT = 48 * 1024 * 1024

BF16 = jnp.bfloat16
F32 = jnp.float32


def _constant_tables():
    half = HALF
    inv_freq = ROPE_BASE ** (-np.arange(half, dtype=np.float64) / half)
    bases = np.concatenate([N_META + TM * np.arange(N_TILES, dtype=np.float64),
                            [-(CHUNK - N_META)]])
    ang_b = bases[:, None] * inv_freq[None, :]
    ang_o = np.arange(TM, dtype=np.float64)[:, None] * inv_freq[None, :]
    cos_b = np.broadcast_to(np.cos(ang_b)[:, None, :], (N_TILES + 1, 8, half))
    sin_b = np.broadcast_to(np.sin(ang_b)[:, None, :], (N_TILES + 1, 8, half))
    gamma = 1.0 - np.exp2(-5.0 - np.arange(RET_HEADS, dtype=np.float64))
    log_g = np.log(gamma)
    idx = np.arange(CHUNK, dtype=np.float64)
    rel = idx[:, None] - idx[None, :]
    scale = HEAD_DIM ** -0.5
    mask = np.where(rel[None] >= 0, np.exp(np.maximum(rel, 0.0)[None] * log_g[:, None, None]), 0.0)
    q_decay = np.exp((idx[None, :] + 1.0) * log_g[:, None])
    k_decay = np.exp((CHUNK - 1.0 - idx[None, :]) * log_g[:, None])
    chunk_decay = np.exp(CHUNK * log_g)
    qdec = np.broadcast_to(q_decay[:, :, None], (RET_HEADS, CHUNK, LANES))
    kdec = np.broadcast_to((k_decay * scale)[:, :, None], (RET_HEADS, CHUNK, LANES))
    f = lambda a: jnp.asarray(np.ascontiguousarray(a), dtype=F32)
    tabs = dict(cos_b=f(cos_b), sin_b=f(sin_b), cos_o=f(np.cos(ang_o)), sin_o=f(np.sin(ang_o)),
                mask=f(mask * scale), qdec=f(qdec), kdec=f(kdec))
    return tabs, tuple(float(c) for c in chunk_decay)


def _rms_norm_rows(xf, g):
    ms = jnp.mean(xf * xf, axis=-1, keepdims=True)
    return xf * lax.rsqrt(ms + EPS) * g


def _rotary(x, cos, sin):
    x1, x2 = x[:, :HALF], x[:, HALF:]
    return jnp.concatenate([x1 * cos - x2 * sin, x1 * sin + x2 * cos], axis=-1)


def _dot(a, b):
    return jnp.dot(a, b, preferred_element_type=F32)


def _dot_nt(a, b):
    return lax.dot_general(a, b, (((1,), (1,)), ((), ())), preferred_element_type=F32)


def _dot_tn(a, b):
    return lax.dot_general(a, b, (((0,), (0,)), ((), ())), preferred_element_type=F32)


def _mix_kernel(chunk_decay,
                x_ref, meta_ref, lng_ref, win_ref, cosb_ref, sinb_ref, cosm_ref, sinm_ref,
                coso_ref, sino_ref, mask_ref, qdec_ref, kdec_ref, gng_ref,
                dww_ref, dwb_ref, clng_ref, clnb_ref, pww_ref, pwb_ref,
                y_ref, state_ref, uext_ref, conv_ref):
    i = pl.program_id(0)

    def state_update(h, k_rot, v):
        kd = (k_rot * jnp.concatenate([kdec_ref[h], kdec_ref[h]], axis=-1)).astype(BF16)
        state_ref[h] = state_ref[h] * chunk_decay[h] + _dot_tn(kd, v.astype(BF16))

    @pl.when(i == 0)
    def _meta():
        state_ref[...] = jnp.zeros_like(state_ref)
        hm = _rms_norm_rows(meta_ref[...], lng_ref[...]).astype(BF16)
        cm, sm = cosm_ref[0][0:1, :], sinm_ref[0][0:1, :]
        cos = cm * coso_ref[:CHUNK] - sm * sino_ref[:CHUNK]
        sin = sm * coso_ref[:CHUNK] + cm * sino_ref[:CHUNK]
        k = _dot(hm, win_ref[:, K0:K0 + RET_WIDTH])
        v = _dot(hm, win_ref[:, V0:V0 + RET_WIDTH])
        for h in range(RET_HEADS):
            hs = slice(h * HEAD_DIM, (h + 1) * HEAD_DIM)
            state_update(h, _rotary(k[:, hs], cos, sin), v[:, hs])
        a = _dot(hm, win_ref[:, A0:A0 + CONV_WIDTH])
        b = _dot(hm, win_ref[:, B0:B0 + CONV_WIDTH])
        u = a * jax.nn.sigmoid(b)
        uext_ref[0:HIST, :] = u[CHUNK - HIST:, :]

    hn = _rms_norm_rows(x_ref[...], lng_ref[...]).astype(BF16)

    cb, sb = cosb_ref[0][0:1, :], sinb_ref[0][0:1, :]
    cos = cb * coso_ref[...] - sb * sino_ref[...]
    sin = sb * coso_ref[...] + cb * sino_ref[...]

    q = _dot(hn, win_ref[:, Q0:Q0 + RET_WIDTH])
    k = _dot(hn, win_ref[:, K0:K0 + RET_WIDTH])
    v = _dot(hn, win_ref[:, V0:V0 + RET_WIDTH])
    gr = _dot(hn, win_ref[:, G0:G0 + RET_WIDTH])
    for h in range(RET_HEADS):
        hs = slice(h * HEAD_DIM, (h + 1) * HEAD_DIM)
        q_rot = _rotary(q[:, hs], cos, sin)
        k_rot = _rotary(k[:, hs], cos, sin)
        qdec = jnp.concatenate([qdec_ref[h], qdec_ref[h]], axis=-1)
        for c in range(TM // CHUNK):
            rs = slice(c * CHUNK, (c + 1) * CHUNK)
            qc, kc, vc = q_rot[rs], k_rot[rs], v[rs, hs]
            scores = _dot_nt(qc.astype(BF16), kc.astype(BF16)) * mask_ref[h]
            inner = _dot(scores.astype(BF16), vc.astype(BF16))
            cross = _dot((qc * qdec).astype(BF16), state_ref[h].astype(BF16))
            state_update(h, kc, vc)
            y = inner + cross
            mu = jnp.mean(y, axis=-1, keepdims=True)
            var = jnp.mean(jnp.square(y - mu), axis=-1, keepdims=True)
            yn = (y - mu) * lax.rsqrt(var + EPS) * gng_ref[:, hs]
            y_ref[rs, hs] = (yn * jax.nn.silu(gr[rs, hs])).astype(y_ref.dtype)

    a = _dot(hn, win_ref[:, A0:A0 + CONV_WIDTH])
    b = _dot(hn, win_ref[:, B0:B0 + CONV_WIDTH])
    uext_ref[HIST:HIST + TM, :] = a * jax.nn.sigmoid(b)

    def conv_block(cbi, carry):
        col = pl.multiple_of(cbi * LANES, LANES)
        for row in range(0, TM, CONV_RB):
            acc = jnp.broadcast_to(dwb_ref[:, pl.ds(col, LANES)], (CONV_RB, LANES))
            for j in range(CONV_K):
                off = row + HIST - (CONV_K - 1) + j
                acc = acc + uext_ref[off:off + CONV_RB, pl.ds(col, LANES)] * \
                    dww_ref[j:j + 1, pl.ds(col, LANES)]
            conv_ref[row:row + CONV_RB, pl.ds(col, LANES)] = acc
        return carry

    lax.fori_loop(0, CONV_WIDTH // LANES, conv_block, 0)
    uext_ref[0:HIST, :] = uext_ref[TM:TM + HIST, :]

    uf = conv_ref[...]
    mu = jnp.mean(uf, axis=-1, keepdims=True)
    var = jnp.mean(jnp.square(uf - mu), axis=-1, keepdims=True)
    uf = (uf - mu) * lax.rsqrt(var + EPS) * clng_ref[...] + clnb_ref[...]
    uc = _dot(jax.nn.silu(uf).astype(BF16), pww_ref[...]) + pwb_ref[...]
    gc = _dot(hn, win_ref[:, C0:C0 + CONV_WIDTH])
    y_ref[:, RET_WIDTH:] = (uc * jax.nn.silu(gc)).astype(y_ref.dtype)


def _out_kernel(y_ref, x_ref, wout_ref, fg_ref, o_ref):
    h = x_ref[...] + _dot(y_ref[...], wout_ref[...])
    o_ref[...] = _rms_norm_rows(h, fg_ref[...])


def _resident(shape):
    nd = len(shape)
    return pl.BlockSpec(shape, lambda i: (0,) * nd, pipeline_mode=pl.Buffered(1))


def kernel(x, meta_tokens, ln_g, w_in, ret_gn_g, conv_dw_w, conv_dw_b, conv_ln_g, conv_ln_b,
           conv_pw_w, conv_pw_b, w_out, final_g):
    assert x.shape == (1, SEQ, D_MODEL) and w_in.shape == (1, D_MODEL, IN_WIDTH)
    tabs, chunk_decay = _constant_tables()
    x2 = x[0]
    meta_pad = jnp.pad(meta_tokens, ((CHUNK - N_META, 0), (0, 0)))
    row = lambda a: a.reshape(1, -1)

    mix_in = [
        (x2, pl.BlockSpec((TM, D_MODEL), lambda i: (i, 0))),
        (meta_pad, _resident((CHUNK, D_MODEL))),
        (row(ln_g[0]), _resident((1, D_MODEL))),
        (w_in[0].astype(BF16), _resident((D_MODEL, IN_WIDTH))),
        (tabs["cos_b"], pl.BlockSpec((1, 8, HALF), lambda i: (i, 0, 0))),
        (tabs["sin_b"], pl.BlockSpec((1, 8, HALF), lambda i: (i, 0, 0))),
        (tabs["cos_b"], pl.BlockSpec((1, 8, HALF), lambda i: (N_TILES, 0, 0))),
        (tabs["sin_b"], pl.BlockSpec((1, 8, HALF), lambda i: (N_TILES, 0, 0))),
        (tabs["cos_o"], _resident((TM, HALF))),
        (tabs["sin_o"], _resident((TM, HALF))),
        (tabs["mask"], _resident((RET_HEADS, CHUNK, CHUNK))),
        (tabs["qdec"], _resident((RET_HEADS, CHUNK, LANES))),
        (tabs["kdec"], _resident((RET_HEADS, CHUNK, LANES))),
        (row(ret_gn_g[0]), _resident((1, RET_WIDTH))),
        (conv_dw_w[0], _resident((CONV_K, CONV_WIDTH))),
        (row(conv_dw_b[0]), _resident((1, CONV_WIDTH))),
        (row(conv_ln_g[0]), _resident((1, CONV_WIDTH))),
        (row(conv_ln_b[0]), _resident((1, CONV_WIDTH))),
        (conv_pw_w[0].astype(BF16), _resident((CONV_WIDTH, CONV_WIDTH))),
        (row(conv_pw_b[0]), _resident((1, CONV_WIDTH))),
    ]
    y = pl.pallas_call(
        functools.partial(_mix_kernel, chunk_decay),
        grid=(N_TILES,),
        in_specs=[s for _, s in mix_in],
        out_specs=pl.BlockSpec((TM, D_MODEL), lambda i: (i, 0)),
        out_shape=jax.ShapeDtypeStruct((SEQ, D_MODEL), BF16),
        scratch_shapes=[
            pltpu.VMEM((RET_HEADS, HEAD_DIM, HEAD_DIM), F32),
            pltpu.VMEM((HIST + TM, CONV_WIDTH), F32),
            pltpu.VMEM((TM, CONV_WIDTH), F32),
        ],
        compiler_params=pltpu.CompilerParams(
            dimension_semantics=("arbitrary",), vmem_limit_bytes=VMEM_LIMIT_MIX),
        name="mix",
    )(*[a for a, _ in mix_in])

    out = pl.pallas_call(
        _out_kernel,
        grid=(SEQ // TM_OUT,),
        in_specs=[
            pl.BlockSpec((TM_OUT, D_MODEL), lambda i: (i, 0)),
            pl.BlockSpec((TM_OUT, D_MODEL), lambda i: (i, 0)),
            _resident((D_MODEL, D_MODEL)),
            _resident((1, D_MODEL)),
        ],
        out_specs=pl.BlockSpec((TM_OUT, D_MODEL), lambda i: (i, 0)),
        out_shape=jax.ShapeDtypeStruct((SEQ, D_MODEL), x.dtype),
        compiler_params=pltpu.CompilerParams(
            dimension_semantics=("arbitrary",), vmem_limit_bytes=VMEM_LIMIT_OUT),
        name="out",
    )(y, x2, w_out[0].astype(BF16), row(final_g))
    return out[None]
```

```python
import functools

import numpy as np
import jax
import jax.numpy as jnp
from jax import lax
from jax.experimental import pallas as pl
from jax.experimental.pallas import tpu as pltpu

D_MODEL = 2048
SEQ = 8192
N_META = 16
RET_WIDTH = 1024
RET_HEADS = 4
HEAD_DIM = 256
HALF = HEAD_DIM // 2
CONV_WIDTH = 1024
CONV_K = 31
CHUNK = 128
ROPE_BASE = 10000.0
EPS = 1e-6
IN_WIDTH = 4 * RET_WIDTH + 3 * CONV_WIDTH

Q0, K0, V0, G0, A0, B0, C0 = 0, 1024, 2048, 3072, 4096, 5120, 6144

LANES = 128
SUBLANES = 8
TM = 256
N_TILES = SEQ // TM
HIST = 32
TM_OUT = 512
CONV_RB = 128
VMEM_LIMIT_MIX = 58 * 1024 * 1024
VMEM_LIMIT_OUT = 48 * 1024 * 1024

BF16 = jnp.bfloat16
F32 = jnp.float32


def _constant_tables():
    half = HALF
    inv_freq = ROPE_BASE ** (-np.arange(half, dtype=np.float64) / half)
    bases = np.concatenate([N_META + TM * np.arange(N_TILES, dtype=np.float64),
                            [-(CHUNK - N_META)]])
    ang_b = bases[:, None] * inv_freq[None, :]
    ang_o = np.arange(TM, dtype=np.float64)[:, None] * inv_freq[None, :]
    cos_b = np.broadcast_to(np.cos(ang_b)[:, None, :], (N_TILES + 1, 8, half))
    sin_b = np.broadcast_to(np.sin(ang_b)[:, None, :], (N_TILES + 1, 8, half))
    gamma = 1.0 - np.exp2(-5.0 - np.arange(RET_HEADS, dtype=np.float64))
    log_g = np.log(gamma)
    idx = np.arange(CHUNK, dtype=np.float64)
    rel = idx[:, None] - idx[None, :]
    scale = HEAD_DIM ** -0.5
    mask = np.where(rel[None] >= 0, np.exp(np.maximum(rel, 0.0)[None] * log_g[:, None, None]), 0.0)
    q_decay = np.exp((idx[None, :] + 1.0) * log_g[:, None])
    k_decay = np.exp((CHUNK - 1.0 - idx[None, :]) * log_g[:, None])
    chunk_decay = np.exp(CHUNK * log_g)
    qdec = np.broadcast_to(q_decay[:, :, None], (RET_HEADS, CHUNK, LANES))
    kdec = np.broadcast_to((k_decay * scale)[:, :, None], (RET_HEADS, CHUNK, LANES))
    f = lambda a: jnp.asarray(np.ascontiguousarray(a), dtype=F32)
    tabs = dict(cos_b=f(cos_b), sin_b=f(sin_b), cos_o=f(np.cos(ang_o)), sin_o=f(np.sin(ang_o)),
                mask=f(mask * scale), qdec=f(qdec), kdec=f(kdec))
    return tabs, tuple(float(c) for c in chunk_decay)


def _rms_norm_rows(xf, g):
    ms = jnp.mean(xf * xf, axis=-1, keepdims=True)
    return xf * lax.rsqrt(ms + EPS) * g


def _rotary(x, cos, sin):
    x1, x2 = x[:, :HALF], x[:, HALF:]
    return jnp.concatenate([x1 * cos - x2 * sin, x1 * sin + x2 * cos], axis=-1)


def _dot(a, b):
    return jnp.dot(a, b, preferred_element_type=F32)


def _dot_nt(a, b):
    return lax.dot_general(a, b, (((1,), (1,)), ((), ())), preferred_element_type=F32)


def _dot_tn(a, b):
    return lax.dot_general(a, b, (((0,), (0,)), ((), ())), preferred_element_type=F32)


def _mix_kernel(chunk_decay,
                x_ref, meta_ref, lng_ref, win_ref, cosb_ref, sinb_ref, cosm_ref, sinm_ref,
                coso_ref, sino_ref, mask_ref, qdec_ref, kdec_ref, gng_ref,
                dww_ref, dwb_ref, clng_ref, clnb_ref, pww_ref, pwb_ref,
                y_ref, state_ref, uph_ref, conv_ref, hn_ref):
    i = pl.program_id(0)
    n_phase = uph_ref.shape[0]

    def state_update(h, k_rot, v):
        kd = (k_rot * jnp.concatenate([kdec_ref[h], kdec_ref[h]], axis=-1)).astype(BF16)
        state_ref[h] = state_ref[h] * chunk_decay[h] + _dot_tn(kd, v.astype(BF16))

    @pl.when(i == 0)
    def _meta():
        state_ref[...] = jnp.zeros_like(state_ref)
        hn_ref[0:CHUNK, :] = _rms_norm_rows(meta_ref[...], lng_ref[...]).astype(BF16)
        hm = hn_ref[0:CHUNK, :]
        cm, sm = cosm_ref[0][0:1, :], sinm_ref[0][0:1, :]
        cos = cm * coso_ref[:CHUNK] - sm * sino_ref[:CHUNK]
        sin = sm * coso_ref[:CHUNK] + cm * sino_ref[:CHUNK]
        k = _dot(hm, win_ref[:, K0:K0 + RET_WIDTH])
        v = _dot(hm, win_ref[:, V0:V0 + RET_WIDTH])
        for h in range(RET_HEADS):
            hs = slice(h * HEAD_DIM, (h + 1) * HEAD_DIM)
            state_update(h, _rotary(k[:, hs], cos, sin), v[:, hs])
        a = _dot(hm, win_ref[:, A0:A0 + CONV_WIDTH])
        b = _dot(hm, win_ref[:, B0:B0 + CONV_WIDTH])
        u = a * jax.nn.sigmoid(b)
        for p in range(n_phase):
            uph_ref[p, 0:HIST - p, :] = u[CHUNK - HIST + p:, :]

    hn_ref[...] = _rms_norm_rows(x_ref[...], lng_ref[...]).astype(BF16)
    hn = hn_ref[...]

    cb, sb = cosb_ref[0][0:1, :], sinb_ref[0][0:1, :]
    cos = cb * coso_ref[...] - sb * sino_ref[...]
    sin = sb * coso_ref[...] + cb * sino_ref[...]

    a = _dot(hn, win_ref[:, A0:A0 + CONV_WIDTH])
    b = _dot(hn, win_ref[:, B0:B0 + CONV_WIDTH])
    u = a * jax.nn.sigmoid(b)
    for p in range(n_phase):
        uph_ref[p, HIST - p:HIST - p + TM, :] = u
    shift = HIST - (CONV_K - 1)
    for cbi in range(CONV_WIDTH // LANES):
        cs = slice(cbi * LANES, (cbi + 1) * LANES)
        for row in range(0, TM, CONV_RB):
            acc = jnp.broadcast_to(dwb_ref[:, cs], (CONV_RB, LANES))
            for p in range(n_phase):
                taps = [j for j in range(CONV_K) if (j + shift) % n_phase == p]
                a0 = min((j + shift) // n_phase for j in taps)
                a1 = max((j + shift) // n_phase for j in taps)
                start = row + n_phase * a0
                win = uph_ref[p, start:start + CONV_RB + n_phase * (a1 - a0), cs]
                for j in taps:
                    o = n_phase * ((j + shift) // n_phase - a0)
                    acc = acc + win[o:o + CONV_RB] * dww_ref[j:j + 1, cs]
            conv_ref[row:row + CONV_RB, cs] = acc
    for p in range(n_phase):
        uph_ref[p, 0:HIST - p, :] = uph_ref[p, TM:TM + HIST - p, :]

    q = _dot(hn, win_ref[:, Q0:Q0 + RET_WIDTH])
    k = _dot(hn, win_ref[:, K0:K0 + RET_WIDTH])
    v = _dot(hn, win_ref[:, V0:V0 + RET_WIDTH])
    gr = _dot(hn, win_ref[:, G0:G0 + RET_WIDTH])
    for h in range(RET_HEADS):
        hs = slice(h * HEAD_DIM, (h + 1) * HEAD_DIM)
        q_rot = _rotary(q[:, hs], cos, sin)
        k_rot = _rotary(k[:, hs], cos, sin)
        qdec = jnp.concatenate([qdec_ref[h], qdec_ref[h]], axis=-1)
        for c in range(TM // CHUNK):
            rs = slice(c * CHUNK, (c + 1) * CHUNK)
            qc, kc, vc = q_rot[rs], k_rot[rs], v[rs, hs]
            scores = _dot_nt(qc.astype(BF16), kc.astype(BF16)) * mask_ref[h]
            inner = _dot(scores.astype(BF16), vc.astype(BF16))
            cross = _dot((qc * qdec).astype(BF16), state_ref[h].astype(BF16))
            state_update(h, kc, vc)
            y = inner + cross
            mu = jnp.mean(y, axis=-1, keepdims=True)
            var = jnp.mean(jnp.square(y - mu), axis=-1, keepdims=True)
            yn = (y - mu) * lax.rsqrt(var + EPS) * gng_ref[:, hs]
            y_ref[rs, hs] = (yn * jax.nn.silu(gr[rs, hs])).astype(y_ref.dtype)

    uf = conv_ref[...]
    mu = jnp.mean(uf, axis=-1, keepdims=True)
    var = jnp.mean(jnp.square(uf - mu), axis=-1, keepdims=True)
    uf = (uf - mu) * lax.rsqrt(var + EPS) * clng_ref[...] + clnb_ref[...]
    uc = _dot(jax.nn.silu(uf).astype(BF16), pww_ref[...]) + pwb_ref[...]
    gc = _dot(hn, win_ref[:, C0:C0 + CONV_WIDTH])
    y_ref[:, RET_WIDTH:] = (uc * jax.nn.silu(gc)).astype(y_ref.dtype)


def _out_kernel(y_ref, x_ref, wout_ref, fg_ref, o_ref):
    h = x_ref[...] + _dot(y_ref[...], wout_ref[...])
    o_ref[...] = _rms_norm_rows(h, fg_ref[...])


def _resident(shape):
    nd = len(shape)
    return pl.BlockSpec(shape, lambda i: (0,) * nd, pipeline_mode=pl.Buffered(1))


def kernel(x, meta_tokens, ln_g, w_in, ret_gn_g, conv_dw_w, conv_dw_b, conv_ln_g, conv_ln_b,
           conv_pw_w, conv_pw_b, w_out, final_g):
    assert x.shape == (1, SEQ, D_MODEL) and w_in.shape == (1, D_MODEL, IN_WIDTH)
    tabs, chunk_decay = _constant_tables()
    x2 = x[0]
    meta_pad = jnp.pad(meta_tokens, ((CHUNK - N_META, 0), (0, 0)))
    row = lambda a: a.reshape(1, -1)

    mix_in = [
        (x2, pl.BlockSpec((TM, D_MODEL), lambda i: (i, 0))),
        (meta_pad, _resident((CHUNK, D_MODEL))),
        (row(ln_g[0]), _resident((1, D_MODEL))),
        (w_in[0].astype(BF16), _resident((D_MODEL, IN_WIDTH))),
        (tabs["cos_b"], pl.BlockSpec((1, 8, HALF), lambda i: (i, 0, 0))),
        (tabs["sin_b"], pl.BlockSpec((1, 8, HALF), lambda i: (i, 0, 0))),
        (tabs["cos_b"], pl.BlockSpec((1, 8, HALF), lambda i: (N_TILES, 0, 0))),
        (tabs["sin_b"], pl.BlockSpec((1, 8, HALF), lambda i: (N_TILES, 0, 0))),
        (tabs["cos_o"], _resident((TM, HALF))),
        (tabs["sin_o"], _resident((TM, HALF))),
        (tabs["mask"], _resident((RET_HEADS, CHUNK, CHUNK))),
        (tabs["qdec"], _resident((RET_HEADS, CHUNK, LANES))),
        (tabs["kdec"], _resident((RET_HEADS, CHUNK, LANES))),
        (row(ret_gn_g[0]), _resident((1, RET_WIDTH))),
        (conv_dw_w[0], _resident((CONV_K, CONV_WIDTH))),
        (row(conv_dw_b[0]), _resident((1, CONV_WIDTH))),
        (row(conv_ln_g[0]), _resident((1, CONV_WIDTH))),
        (row(conv_ln_b[0]), _resident((1, CONV_WIDTH))),
        (conv_pw_w[0].astype(BF16), _resident((CONV_WIDTH, CONV_WIDTH))),
        (row(conv_pw_b[0]), _resident((1, CONV_WIDTH))),
    ]
    y = pl.pallas_call(
        functools.partial(_mix_kernel, chunk_decay),
        grid=(N_TILES,),
        in_specs=[s for _, s in mix_in],
        out_specs=pl.BlockSpec((TM, D_MODEL), lambda i: (i, 0)),
        out_shape=jax.ShapeDtypeStruct((SEQ, D_MODEL), BF16),
        scratch_shapes=[
            pltpu.VMEM((RET_HEADS, HEAD_DIM, HEAD_DIM), F32),
            pltpu.VMEM((SUBLANES, HIST + TM, CONV_WIDTH), F32),
            pltpu.VMEM((TM, CONV_WIDTH), F32),
            pltpu.VMEM((TM, D_MODEL), BF16),
        ],
        compiler_params=pltpu.CompilerParams(
            dimension_semantics=("arbitrary",), vmem_limit_bytes=VMEM_LIMIT_MIX),
        name="mix",
    )(*[a for a, _ in mix_in])

    out = pl.pallas_call(
        _out_kernel,
        grid=(SEQ // TM_OUT,),
        in_specs=[
            pl.BlockSpec((TM_OUT, D_MODEL), lambda i: (i, 0)),
            pl.BlockSpec((TM_OUT, D_MODEL), lambda i: (i, 0)),
            _resident((D_MODEL, D_MODEL)),
            _resident((1, D_MODEL)),
        ],
        out_specs=pl.BlockSpec((TM_OUT, D_MODEL), lambda i: (i, 0)),
        out_shape=jax.ShapeDtypeStruct((SEQ, D_MODEL), x.dtype),
        compiler_params=pltpu.CompilerParams(
            dimension_semantics=("arbitrary",), vmem_limit_bytes=VMEM_LIMIT_OUT),
        name="out",
    )(y, x2, w_out[0].astype(BF16), row(final_g))
    return out[None]
```

```python
import functools

import numpy as np
import jax
import jax.numpy as jnp
from jax import lax
from jax.experimental import pallas as pl
from jax.experimental.pallas import tpu as pltpu

D_MODEL = 2048
SEQ = 8192
N_META = 16
RET_WIDTH = 1024
RET_HEADS = 4
HEAD_DIM = 256
HALF = HEAD_DIM // 2
CONV_WIDTH = 1024
CONV_K = 31
CHUNK = 128
ROPE_BASE = 10000.0
EPS = 1e-6
IN_WIDTH = 4 * RET_WIDTH + 3 * CONV_WIDTH

Q0, K0, V0, G0, A0, B0, C0 = 0, 1024, 2048, 3072, 4096, 5120, 6144

LANES = 128
SUBLANES = 8
TM = 256
N_TILES = SEQ // TM
HIST = 32
TM_OUT = 512
CONV_RB = 128
STAGE_ROWS = 64
VMEM_LIMIT_MIX = 58 * 1024 * 1024
VMEM_LIMIT_OUT = 48 * 1024 * 1024

BF16 = jnp.bfloat16
F32 = jnp.float32


def _constant_tables():
    half = HALF
    inv_freq = ROPE_BASE ** (-np.arange(half, dtype=np.float64) / half)
    bases = np.concatenate([N_META + TM * np.arange(N_TILES, dtype=np.float64),
                            [-(CHUNK - N_META)]])
    ang_b = bases[:, None] * inv_freq[None, :]
    ang_o = np.arange(TM, dtype=np.float64)[:, None] * inv_freq[None, :]
    cos_b = np.broadcast_to(np.cos(ang_b)[:, None, :], (N_TILES + 1, 8, half))
    sin_b = np.broadcast_to(np.sin(ang_b)[:, None, :], (N_TILES + 1, 8, half))
    gamma = 1.0 - np.exp2(-5.0 - np.arange(RET_HEADS, dtype=np.float64))
    log_g = np.log(gamma)
    idx = np.arange(CHUNK, dtype=np.float64)
    rel = idx[:, None] - idx[None, :]
    scale = HEAD_DIM ** -0.5
    mask = np.where(rel[None] >= 0, np.exp(np.maximum(rel, 0.0)[None] * log_g[:, None, None]), 0.0)
    q_decay = np.exp((idx[None, :] + 1.0) * log_g[:, None])
    k_decay = np.exp((CHUNK - 1.0 - idx[None, :]) * log_g[:, None])
    chunk_decay = np.exp(CHUNK * log_g)
    qdec = np.broadcast_to(q_decay[:, :, None], (RET_HEADS, CHUNK, LANES))
    kdec = np.broadcast_to((k_decay * scale)[:, :, None], (RET_HEADS, CHUNK, LANES))
    f = lambda a: jnp.asarray(np.ascontiguousarray(a), dtype=F32)
    tabs = dict(cos_b=f(cos_b), sin_b=f(sin_b), cos_o=f(np.cos(ang_o)), sin_o=f(np.sin(ang_o)),
                mask=f(mask * scale), qdec=f(qdec), kdec=f(kdec))
    return tabs, tuple(float(c) for c in chunk_decay)


def _rms_norm_rows(xf, g):
    ms = jnp.mean(xf * xf, axis=-1, keepdims=True)
    return xf * lax.rsqrt(ms + EPS) * g


def _rotary(x, cos, sin):
    x1, x2 = x[:, :HALF], x[:, HALF:]
    return jnp.concatenate([x1 * cos - x2 * sin, x1 * sin + x2 * cos], axis=-1)


def _dot(a, b):
    return jnp.dot(a, b, preferred_element_type=F32)


def _dot_nt(a, b):
    return lax.dot_general(a, b, (((1,), (1,)), ((), ())), preferred_element_type=F32)


def _dot_tn(a, b):
    return lax.dot_general(a, b, (((0,), (0,)), ((), ())), preferred_element_type=F32)


def _load_cast_weight(w_hbm, w_vmem, stage_ref, sem_ref):
    rows = stage_ref.shape[1]
    n_chunks = w_hbm.shape[0] // rows

    def chunk_copy(c, slot):
        return pltpu.make_async_copy(w_hbm.at[pl.ds(c * rows, rows), :], stage_ref.at[slot],
                                     sem_ref.at[slot])

    chunk_copy(0, 0).start()

    def body(c, carry):
        slot = c % 2
        chunk_copy(c, slot).wait()

        @pl.when(c + 1 < n_chunks)
        def _prefetch():
            chunk_copy(c + 1, 1 - slot).start()

        r0 = pl.multiple_of(c * rows, rows)
        w_vmem[pl.ds(r0, rows), :] = stage_ref[slot].astype(BF16)
        return carry

    lax.fori_loop(0, n_chunks, body, 0)


def _mix_kernel(chunk_decay,
                x_ref, meta_ref, lng_ref, win_hbm, cosb_ref, sinb_ref, cosm_ref, sinm_ref,
                coso_ref, sino_ref, mask_ref, qdec_ref, kdec_ref, gng_ref,
                dww_ref, dwb_ref, clng_ref, clnb_ref, pww_hbm, pwb_ref,
                y_ref,
                win_ref, pww_ref, stage_in_ref, stage_pw_ref, sem_in_ref, sem_pw_ref,
                state_ref, uph_ref, conv_ref, hn_ref):
    i = pl.program_id(0)
    n_phase = uph_ref.shape[0]

    def state_update(h, k_rot, v):
        kd = (k_rot * jnp.concatenate([kdec_ref[h], kdec_ref[h]], axis=-1)).astype(BF16)
        state_ref[h] = state_ref[h] * chunk_decay[h] + _dot_tn(kd, v.astype(BF16))

    @pl.when(i == 0)
    def _meta():
        _load_cast_weight(win_hbm, win_ref, stage_in_ref, sem_in_ref)
        _load_cast_weight(pww_hbm, pww_ref, stage_pw_ref, sem_pw_ref)
        state_ref[...] = jnp.zeros_like(state_ref)
        hn_ref[0:CHUNK - N_META, :] = jnp.zeros((CHUNK - N_META, D_MODEL), BF16)
        hn_ref[CHUNK - N_META:CHUNK, :] = _rms_norm_rows(meta_ref[...], lng_ref[...]).astype(BF16)
        hm = hn_ref[0:CHUNK, :]
        cm, sm = cosm_ref[0][0:1, :], sinm_ref[0][0:1, :]
        cos = cm * coso_ref[:CHUNK] - sm * sino_ref[:CHUNK]
        sin = sm * coso_ref[:CHUNK] + cm * sino_ref[:CHUNK]
        k = _dot(hm, win_ref[:, K0:K0 + RET_WIDTH])
        v = _dot(hm, win_ref[:, V0:V0 + RET_WIDTH])
        for h in range(RET_HEADS):
            hs = slice(h * HEAD_DIM, (h + 1) * HEAD_DIM)
            state_update(h, _rotary(k[:, hs], cos, sin), v[:, hs])
        a = _dot(hm, win_ref[:, A0:A0 + CONV_WIDTH])
        b = _dot(hm, win_ref[:, B0:B0 + CONV_WIDTH])
        u = a * jax.nn.sigmoid(b)
        for p in range(n_phase):
            uph_ref[p, 0:HIST - p, :] = u[CHUNK - HIST + p:, :]

    hn_ref[...] = _rms_norm_rows(x_ref[...], lng_ref[...]).astype(BF16)
    hn = hn_ref[...]

    cb, sb = cosb_ref[0][0:1, :], sinb_ref[0][0:1, :]
    cos = cb * coso_ref[...] - sb * sino_ref[...]
    sin = sb * coso_ref[...] + cb * sino_ref[...]

    a = _dot(hn, win_ref[:, A0:A0 + CONV_WIDTH])
    b = _dot(hn, win_ref[:, B0:B0 + CONV_WIDTH])
    u = a * jax.nn.sigmoid(b)
    shift = HIST - (CONV_K - 1)
    for row in range(0, TM, CONV_RB):
        for p in range(n_phase):
            uph_ref[p, HIST - p:HIST - p + CONV_RB, :] = u[row:row + CONV_RB]
        for cbi in range(CONV_WIDTH // LANES):
            cs = slice(cbi * LANES, (cbi + 1) * LANES)
            acc = jnp.broadcast_to(dwb_ref[:, cs], (CONV_RB, LANES))
            for p in range(n_phase):
                taps = [j for j in range(CONV_K) if (j + shift) % n_phase == p]
                a0 = min((j + shift) // n_phase for j in taps)
                a1 = max((j + shift) // n_phase for j in taps)
                start = n_phase * a0
                win = uph_ref[p, start:start + CONV_RB + n_phase * (a1 - a0), cs]
                for j in taps:
                    o = n_phase * ((j + shift) // n_phase - a0)
                    acc = acc + win[o:o + CONV_RB] * dww_ref[j:j + 1, cs]
            conv_ref[row:row + CONV_RB, cs] = acc
        for p in range(n_phase):
            uph_ref[p, 0:HIST - p, :] = uph_ref[p, CONV_RB:CONV_RB + HIST - p, :]

    q = _dot(hn, win_ref[:, Q0:Q0 + RET_WIDTH])
    k = _dot(hn, win_ref[:, K0:K0 + RET_WIDTH])
    v = _dot(hn, win_ref[:, V0:V0 + RET_WIDTH])
    gr = _dot(hn, win_ref[:, G0:G0 + RET_WIDTH])
    for h in range(RET_HEADS):
        hs = slice(h * HEAD_DIM, (h + 1) * HEAD_DIM)
        q_rot = _rotary(q[:, hs], cos, sin)
        k_rot = _rotary(k[:, hs], cos, sin)
        qdec = jnp.concatenate([qdec_ref[h], qdec_ref[h]], axis=-1)
        for c in range(TM // CHUNK):
            rs = slice(c * CHUNK, (c + 1) * CHUNK)
            qc, kc, vc = q_rot[rs], k_rot[rs], v[rs, hs]
            scores = _dot_nt(qc.astype(BF16), kc.astype(BF16)) * mask_ref[h]
            inner = _dot(scores.astype(BF16), vc.astype(BF16))
            cross = _dot((qc * qdec).astype(BF16), state_ref[h].astype(BF16))
            state_update(h, kc, vc)
            y = inner + cross
            mu = jnp.mean(y, axis=-1, keepdims=True)
            var = jnp.mean(jnp.square(y - mu), axis=-1, keepdims=True)
            yn = (y - mu) * lax.rsqrt(var + EPS) * gng_ref[:, hs]
            y_ref[rs, hs] = (yn * jax.nn.silu(gr[rs, hs])).astype(y_ref.dtype)

    uf = conv_ref[...]
    mu = jnp.mean(uf, axis=-1, keepdims=True)
    var = jnp.mean(jnp.square(uf - mu), axis=-1, keepdims=True)
    uf = (uf - mu) * lax.rsqrt(var + EPS) * clng_ref[...] + clnb_ref[...]
    uc = _dot(jax.nn.silu(uf).astype(BF16), pww_ref[...]) + pwb_ref[...]
    gc = _dot(hn, win_ref[:, C0:C0 + CONV_WIDTH])
    y_ref[:, RET_WIDTH:] = (uc * jax.nn.silu(gc)).astype(y_ref.dtype)


def _out_kernel(y_ref, x_ref, wout_hbm, fg_ref, o_ref, wout_ref, stage_ref, sem_ref):
    @pl.when(pl.program_id(0) == 0)
    def _weights():
        _load_cast_weight(wout_hbm, wout_ref, stage_ref, sem_ref)

    h = x_ref[...] + _dot(y_ref[...], wout_ref[...])
    o_ref[...] = _rms_norm_rows(h, fg_ref[...])


def _resident(shape):
    nd = len(shape)
    return pl.BlockSpec(shape, lambda i: (0,) * nd, pipeline_mode=pl.Buffered(1))


def kernel(x, meta_tokens, ln_g, w_in, ret_gn_g, conv_dw_w, conv_dw_b, conv_ln_g, conv_ln_b,
           conv_pw_w, conv_pw_b, w_out, final_g):
    assert x.shape == (1, SEQ, D_MODEL) and w_in.shape == (1, D_MODEL, IN_WIDTH)
    tabs, chunk_decay = _constant_tables()
    x2 = x[0]
    row = lambda a: a.reshape(1, -1)
    hbm = pl.BlockSpec(memory_space=pl.ANY)

    mix_in = [
        (x2, pl.BlockSpec((TM, D_MODEL), lambda i: (i, 0))),
        (meta_tokens, _resident((N_META, D_MODEL))),
        (row(ln_g[0]), _resident((1, D_MODEL))),
        (w_in[0], hbm),
        (tabs["cos_b"], pl.BlockSpec((1, 8, HALF), lambda i: (i, 0, 0))),
        (tabs["sin_b"], pl.BlockSpec((1, 8, HALF), lambda i: (i, 0, 0))),
        (tabs["cos_b"], pl.BlockSpec((1, 8, HALF), lambda i: (N_TILES, 0, 0))),
        (tabs["sin_b"], pl.BlockSpec((1, 8, HALF), lambda i: (N_TILES, 0, 0))),
        (tabs["cos_o"], _resident((TM, HALF))),
        (tabs["sin_o"], _resident((TM, HALF))),
        (tabs["mask"], _resident((RET_HEADS, CHUNK, CHUNK))),
        (tabs["qdec"], _resident((RET_HEADS, CHUNK, LANES))),
        (tabs["kdec"], _resident((RET_HEADS, CHUNK, LANES))),
        (row(ret_gn_g[0]), _resident((1, RET_WIDTH))),
        (conv_dw_w[0], _resident((CONV_K, CONV_WIDTH))),
        (row(conv_dw_b[0]), _resident((1, CONV_WIDTH))),
        (row(conv_ln_g[0]), _resident((1, CONV_WIDTH))),
        (row(conv_ln_b[0]), _resident((1, CONV_WIDTH))),
        (conv_pw_w[0], hbm),
        (row(conv_pw_b[0]), _resident((1, CONV_WIDTH))),
    ]
    y = pl.pallas_call(
        functools.partial(_mix_kernel, chunk_decay),
        grid=(N_TILES,),
        in_specs=[s for _, s in mix_in],
        out_specs=pl.BlockSpec((TM, D_MODEL), lambda i: (i, 0)),
        out_shape=jax.ShapeDtypeStruct((SEQ, D_MODEL), BF16),
        scratch_shapes=[
            pltpu.VMEM((D_MODEL, IN_WIDTH), BF16),
            pltpu.VMEM((CONV_WIDTH, CONV_WIDTH), BF16),
            pltpu.VMEM((2, STAGE_ROWS, IN_WIDTH), F32),
            pltpu.VMEM((2, STAGE_ROWS, CONV_WIDTH), F32),
            pltpu.SemaphoreType.DMA((2,)),
            pltpu.SemaphoreType.DMA((2,)),
            pltpu.VMEM((RET_HEADS, HEAD_DIM, HEAD_DIM), F32),
            pltpu.VMEM((SUBLANES, HIST + CONV_RB, CONV_WIDTH), F32),
            pltpu.VMEM((TM, CONV_WIDTH), F32),
            pltpu.VMEM((TM, D_MODEL), BF16),
        ],
        compiler_params=pltpu.CompilerParams(
            dimension_semantics=("arbitrary",), vmem_limit_bytes=VMEM_LIMIT_MIX),
        name="mix",
    )(*[a for a, _ in mix_in])

    out = pl.pallas_call(
        _out_kernel,
        grid=(SEQ // TM_OUT,),
        in_specs=[
            pl.BlockSpec((TM_OUT, D_MODEL), lambda i: (i, 0)),
            pl.BlockSpec((TM_OUT, D_MODEL), lambda i: (i, 0)),
            hbm,
            _resident((1, D_MODEL)),
        ],
        out_specs=pl.BlockSpec((TM_OUT, D_MODEL), lambda i: (i, 0)),
        out_shape=jax.ShapeDtypeStruct((SEQ, D_MODEL), x.dtype),
        scratch_shapes=[
            pltpu.VMEM((D_MODEL, D_MODEL), BF16),
            pltpu.VMEM((2, STAGE_ROWS, D_MODEL), F32),
            pltpu.SemaphoreType.DMA((2,)),
        ],
        compiler_params=pltpu.CompilerParams(
            dimension_semantics=("arbitrary",), vmem_limit_bytes=VMEM_LIMIT_OUT),
        name="out",
    )(y, x2, w_out[0], row(final_g))
    return out[None]
```

```python
import functools

import numpy as np
import jax
import jax.numpy as jnp
from jax import lax
from jax.experimental import pallas as pl
from jax.experimental.pallas import tpu as pltpu

D_MODEL = 2048
SEQ = 8192
N_META = 16
RET_WIDTH = 1024
RET_HEADS = 4
HEAD_DIM = 256
HALF = HEAD_DIM // 2
CONV_WIDTH = 1024
CONV_K = 31
CHUNK = 128
ROPE_BASE = 10000.0
EPS = 1e-6
IN_WIDTH = 4 * RET_WIDTH + 3 * CONV_WIDTH

Q0, K0, V0, G0, A0, B0, C0 = 0, 1024, 2048, 3072, 4096, 5120, 6144

LANES = 128
SUBLANES = 8
TM = 256
N_TILES = SEQ // TM
HIST = 32
TM_OUT = 512
CONV_RB = 128
STAGE_SLOTS = 4
STAGE_ROWS_IN = 32
STAGE_ROWS_SQ = 128
VMEM_LIMIT_MIX = 58 * 1024 * 1024
VMEM_LIMIT_OUT = 48 * 1024 * 1024

BF16 = jnp.bfloat16
F32 = jnp.float32


def _constant_tables():
    half = HALF
    inv_freq = ROPE_BASE ** (-np.arange(half, dtype=np.float64) / half)
    bases = np.concatenate([N_META + TM * np.arange(N_TILES, dtype=np.float64),
                            [-(CHUNK - N_META)]])
    ang_b = bases[:, None] * inv_freq[None, :]
    ang_o = np.arange(TM, dtype=np.float64)[:, None] * inv_freq[None, :]
    cos_b = np.broadcast_to(np.cos(ang_b)[:, None, :], (N_TILES + 1, 8, half))
    sin_b = np.broadcast_to(np.sin(ang_b)[:, None, :], (N_TILES + 1, 8, half))
    gamma = 1.0 - np.exp2(-5.0 - np.arange(RET_HEADS, dtype=np.float64))
    log_g = np.log(gamma)
    idx = np.arange(CHUNK, dtype=np.float64)
    rel = idx[:, None] - idx[None, :]
    scale = HEAD_DIM ** -0.5
    mask = np.where(rel[None] >= 0, np.exp(np.maximum(rel, 0.0)[None] * log_g[:, None, None]), 0.0)
    q_decay = np.exp((idx[None, :] + 1.0) * log_g[:, None])
    k_decay = np.exp((CHUNK - 1.0 - idx[None, :]) * log_g[:, None])
    chunk_decay = np.exp(CHUNK * log_g)
    qdec = np.broadcast_to(q_decay[:, :, None], (RET_HEADS, CHUNK, LANES))
    kdec = np.broadcast_to((k_decay * scale)[:, :, None], (RET_HEADS, CHUNK, LANES))
    f = lambda a: jnp.asarray(np.ascontiguousarray(a), dtype=F32)
    tabs = dict(cos_b=f(cos_b), sin_b=f(sin_b), cos_o=f(np.cos(ang_o)), sin_o=f(np.sin(ang_o)),
                mask=f(mask * scale), qdec=f(qdec), kdec=f(kdec))
    return tabs, tuple(float(c) for c in chunk_decay)


def _rms_norm_rows(xf, g):
    ms = jnp.mean(xf * xf, axis=-1, keepdims=True)
    return xf * lax.rsqrt(ms + EPS) * g


def _rotary(x, cos, sin):
    x1, x2 = x[:, :HALF], x[:, HALF:]
    return jnp.concatenate([x1 * cos - x2 * sin, x1 * sin + x2 * cos], axis=-1)


def _dot(a, b):
    return jnp.dot(a, b, preferred_element_type=F32)


def _dot_nt(a, b):
    return lax.dot_general(a, b, (((1,), (1,)), ((), ())), preferred_element_type=F32)


def _dot_tn(a, b):
    return lax.dot_general(a, b, (((0,), (0,)), ((), ())), preferred_element_type=F32)


def _load_cast_weight(w_hbm, w_vmem, stage_ref, sem_ref):
    n_slots, rows = stage_ref.shape[0], stage_ref.shape[1]
    n_chunks = w_hbm.shape[0] // rows

    def chunk_copy(c, slot):
        return pltpu.make_async_copy(w_hbm.at[pl.ds(c * rows, rows), :], stage_ref.at[slot],
                                     sem_ref.at[slot])

    for c in range(n_slots - 1):
        chunk_copy(c, c).start()

    def body(c, carry):
        slot = c % n_slots
        chunk_copy(c, slot).wait()
        nxt = c + n_slots - 1

        @pl.when(nxt < n_chunks)
        def _prefetch():
            chunk_copy(nxt, nxt % n_slots).start()

        r0 = pl.multiple_of(c * rows, rows)
        w_vmem[pl.ds(r0, rows), :] = stage_ref[slot].astype(BF16)
        return carry

    lax.fori_loop(0, n_chunks, body, 0)


def _mix_kernel(chunk_decay,
                x_ref, meta_ref, lng_ref, win_hbm, cosb_ref, sinb_ref, cosm_ref, sinm_ref,
                coso_ref, sino_ref, mask_ref, qdec_ref, kdec_ref, gng_ref,
                dww_ref, dwb_ref, clng_ref, clnb_ref, pww_hbm, pwb_ref,
                y_ref,
                win_ref, pww_ref, stage_in_ref, stage_pw_ref, sem_in_ref, sem_pw_ref,
                state_ref, uph_ref, conv_ref, hn_ref):
    i = pl.program_id(0)
    n_phase = uph_ref.shape[0]

    def state_update(h, k_rot, v):
        kd = (k_rot * jnp.concatenate([kdec_ref[h], kdec_ref[h]], axis=-1)).astype(BF16)
        state_ref[h] = state_ref[h] * chunk_decay[h] + _dot_tn(kd, v.astype(BF16))

    @pl.when(i == 0)
    def _meta():
        _load_cast_weight(win_hbm, win_ref, stage_in_ref, sem_in_ref)
        _load_cast_weight(pww_hbm, pww_ref, stage_pw_ref, sem_pw_ref)
        state_ref[...] = jnp.zeros_like(state_ref)
        hn_ref[0:CHUNK - N_META, :] = jnp.zeros((CHUNK - N_META, D_MODEL), BF16)
        hn_ref[CHUNK - N_META:CHUNK, :] = _rms_norm_rows(meta_ref[...], lng_ref[...]).astype(BF16)
        hm = hn_ref[0:CHUNK, :]
        cm, sm = cosm_ref[0][0:1, :], sinm_ref[0][0:1, :]
        cos = cm * coso_ref[:CHUNK] - sm * sino_ref[:CHUNK]
        sin = sm * coso_ref[:CHUNK] + cm * sino_ref[:CHUNK]
        k = _dot(hm, win_ref[:, K0:K0 + RET_WIDTH])
        v = _dot(hm, win_ref[:, V0:V0 + RET_WIDTH])
        for h in range(RET_HEADS):
            hs = slice(h * HEAD_DIM, (h + 1) * HEAD_DIM)
            state_update(h, _rotary(k[:, hs], cos, sin), v[:, hs])
        a = _dot(hm, win_ref[:, A0:A0 + CONV_WIDTH])
        b = _dot(hm, win_ref[:, B0:B0 + CONV_WIDTH])
        u = a * jax.nn.sigmoid(b)
        for p in range(n_phase):
            uph_ref[p, 0:HIST - p, :] = u[CHUNK - HIST + p:, :]

    hn_ref[...] = _rms_norm_rows(x_ref[...], lng_ref[...]).astype(BF16)
    hn = hn_ref[...]

    cb, sb = cosb_ref[0][0:1, :], sinb_ref[0][0:1, :]
    cos = cb * coso_ref[...] - sb * sino_ref[...]
    sin = sb * coso_ref[...] + cb * sino_ref[...]

    a = _dot(hn, win_ref[:, A0:A0 + CONV_WIDTH])
    b = _dot(hn, win_ref[:, B0:B0 + CONV_WIDTH])
    u = a * jax.nn.sigmoid(b)
    shift = HIST - (CONV_K - 1)
    for row in range(0, TM, CONV_RB):
        for p in range(n_phase):
            uph_ref[p, HIST - p:HIST - p + CONV_RB, :] = u[row:row + CONV_RB]
        for cbi in range(CONV_WIDTH // LANES):
            cs = slice(cbi * LANES, (cbi + 1) * LANES)
            acc = jnp.broadcast_to(dwb_ref[:, cs], (CONV_RB, LANES))
            for p in range(n_phase):
                taps = [j for j in range(CONV_K) if (j + shift) % n_phase == p]
                a0 = min((j + shift) // n_phase for j in taps)
                a1 = max((j + shift) // n_phase for j in taps)
                start = n_phase * a0
                win = uph_ref[p, start:start + CONV_RB + n_phase * (a1 - a0), cs]
                for j in taps:
                    o = n_phase * ((j + shift) // n_phase - a0)
                    acc = acc + win[o:o + CONV_RB] * dww_ref[j:j + 1, cs]
            conv_ref[row:row + CONV_RB, cs] = acc
        for p in range(n_phase):
            uph_ref[p, 0:HIST - p, :] = uph_ref[p, CONV_RB:CONV_RB + HIST - p, :]

    q = _dot(hn, win_ref[:, Q0:Q0 + RET_WIDTH])
    k = _dot(hn, win_ref[:, K0:K0 + RET_WIDTH])
    v = _dot(hn, win_ref[:, V0:V0 + RET_WIDTH])
    gr = _dot(hn, win_ref[:, G0:G0 + RET_WIDTH])
    for h in range(RET_HEADS):
        hs = slice(h * HEAD_DIM, (h + 1) * HEAD_DIM)
        q_rot = _rotary(q[:, hs], cos, sin)
        k_rot = _rotary(k[:, hs], cos, sin)
        qdec = jnp.concatenate([qdec_ref[h], qdec_ref[h]], axis=-1)
        for c in range(TM // CHUNK):
            rs = slice(c * CHUNK, (c + 1) * CHUNK)
            qc, kc, vc = q_rot[rs], k_rot[rs], v[rs, hs]
            scores = _dot_nt(qc.astype(BF16), kc.astype(BF16)) * mask_ref[h]
            inner = _dot(scores.astype(BF16), vc.astype(BF16))
            cross = _dot((qc * qdec).astype(BF16), state_ref[h].astype(BF16))
            state_update(h, kc, vc)
            y = inner + cross
            mu = jnp.mean(y, axis=-1, keepdims=True)
            var = jnp.mean(jnp.square(y - mu), axis=-1, keepdims=True)
            yn = (y - mu) * lax.rsqrt(var + EPS) * gng_ref[:, hs]
            y_ref[rs, hs] = (yn * jax.nn.silu(gr[rs, hs])).astype(y_ref.dtype)

    uf = conv_ref[...]
    mu = jnp.mean(uf, axis=-1, keepdims=True)
    var = jnp.mean(jnp.square(uf - mu), axis=-1, keepdims=True)
    uf = (uf - mu) * lax.rsqrt(var + EPS) * clng_ref[...] + clnb_ref[...]
    uc = _dot(jax.nn.silu(uf).astype(BF16), pww_ref[...]) + pwb_ref[...]
    gc = _dot(hn, win_ref[:, C0:C0 + CONV_WIDTH])
    y_ref[:, RET_WIDTH:] = (uc * jax.nn.silu(gc)).astype(y_ref.dtype)


def _out_kernel(y_ref, x_ref, wout_hbm, fg_ref, o_ref, wout_ref, stage_ref, sem_ref):
    @pl.when(pl.program_id(0) == 0)
    def _weights():
        _load_cast_weight(wout_hbm, wout_ref, stage_ref, sem_ref)

    h = x_ref[...] + _dot(y_ref[...], wout_ref[...])
    o_ref[...] = _rms_norm_rows(h, fg_ref[...])


def _resident(shape):
    nd = len(shape)
    return pl.BlockSpec(shape, lambda i: (0,) * nd, pipeline_mode=pl.Buffered(1))


def kernel(x, meta_tokens, ln_g, w_in, ret_gn_g, conv_dw_w, conv_dw_b, conv_ln_g, conv_ln_b,
           conv_pw_w, conv_pw_b, w_out, final_g):
    assert x.shape == (1, SEQ, D_MODEL) and w_in.shape == (1, D_MODEL, IN_WIDTH)
    tabs, chunk_decay = _constant_tables()
    x2 = x[0]
    row = lambda a: a.reshape(1, -1)
    hbm = pl.BlockSpec(memory_space=pl.ANY)

    mix_in = [
        (x2, pl.BlockSpec((TM, D_MODEL), lambda i: (i, 0))),
        (meta_tokens, _resident((N_META, D_MODEL))),
        (row(ln_g[0]), _resident((1, D_MODEL))),
        (w_in[0], hbm),
        (tabs["cos_b"], pl.BlockSpec((1, 8, HALF), lambda i: (i, 0, 0))),
        (tabs["sin_b"], pl.BlockSpec((1, 8, HALF), lambda i: (i, 0, 0))),
        (tabs["cos_b"], pl.BlockSpec((1, 8, HALF), lambda i: (N_TILES, 0, 0))),
        (tabs["sin_b"], pl.BlockSpec((1, 8, HALF), lambda i: (N_TILES, 0, 0))),
        (tabs["cos_o"], _resident((TM, HALF))),
        (tabs["sin_o"], _resident((TM, HALF))),
        (tabs["mask"], _resident((RET_HEADS, CHUNK, CHUNK))),
        (tabs["qdec"], _resident((RET_HEADS, CHUNK, LANES))),
        (tabs["kdec"], _resident((RET_HEADS, CHUNK, LANES))),
        (row(ret_gn_g[0]), _resident((1, RET_WIDTH))),
        (conv_dw_w[0], _resident((CONV_K, CONV_WIDTH))),
        (row(conv_dw_b[0]), _resident((1, CONV_WIDTH))),
        (row(conv_ln_g[0]), _resident((1, CONV_WIDTH))),
        (row(conv_ln_b[0]), _resident((1, CONV_WIDTH))),
        (conv_pw_w[0], hbm),
        (row(conv_pw_b[0]), _resident((1, CONV_WIDTH))),
    ]
    y = pl.pallas_call(
        functools.partial(_mix_kernel, chunk_decay),
        grid=(N_TILES,),
        in_specs=[s for _, s in mix_in],
        out_specs=pl.BlockSpec((TM, D_MODEL), lambda i: (i, 0)),
        out_shape=jax.ShapeDtypeStruct((SEQ, D_MODEL), BF16),
        scratch_shapes=[
            pltpu.VMEM((D_MODEL, IN_WIDTH), BF16),
            pltpu.VMEM((CONV_WIDTH, CONV_WIDTH), BF16),
            pltpu.VMEM((STAGE_SLOTS, STAGE_ROWS_IN, IN_WIDTH), F32),
            pltpu.VMEM((STAGE_SLOTS, STAGE_ROWS_SQ, CONV_WIDTH), F32),
            pltpu.SemaphoreType.DMA((STAGE_SLOTS,)),
            pltpu.SemaphoreType.DMA((STAGE_SLOTS,)),
            pltpu.VMEM((RET_HEADS, HEAD_DIM, HEAD_DIM), F32),
            pltpu.VMEM((SUBLANES, HIST + CONV_RB, CONV_WIDTH), F32),
            pltpu.VMEM((TM, CONV_WIDTH), F32),
            pltpu.VMEM((TM, D_MODEL), BF16),
        ],
        compiler_params=pltpu.CompilerParams(
            dimension_semantics=("arbitrary",), vmem_limit_bytes=VMEM_LIMIT_MIX),
        name="mix",
    )(*[a for a, _ in mix_in])

    out = pl.pallas_call(
        _out_kernel,
        grid=(SEQ // TM_OUT,),
        in_specs=[
            pl.BlockSpec((TM_OUT, D_MODEL), lambda i: (i, 0)),
            pl.BlockSpec((TM_OUT, D_MODEL), lambda i: (i, 0)),
            hbm,
            _resident((1, D_MODEL)),
        ],
        out_specs=pl.BlockSpec((TM_OUT, D_MODEL), lambda i: (i, 0)),
        out_shape=jax.ShapeDtypeStruct((SEQ, D_MODEL), x.dtype),
        scratch_shapes=[
            pltpu.VMEM((D_MODEL, D_MODEL), BF16),
            pltpu.VMEM((STAGE_SLOTS, STAGE_ROWS_SQ, D_MODEL), F32),
            pltpu.SemaphoreType.DMA((STAGE_SLOTS,)),
        ],
        compiler_params=pltpu.CompilerParams(
            dimension_semantics=("arbitrary",), vmem_limit_bytes=VMEM_LIMIT_OUT),
        name="out",
    )(y, x2, w_out[0], row(final_g))
    return out[None]
```

```python
import functools

import numpy as np
import jax
import jax.numpy as jnp
from jax import lax
from jax.experimental import pallas as pl
from jax.experimental.pallas import tpu as pltpu

D_MODEL = 2048
SEQ = 8192
N_META = 16
RET_WIDTH = 1024
RET_HEADS = 4
HEAD_DIM = 256
HALF = HEAD_DIM // 2
CONV_WIDTH = 1024
CONV_K = 31
CHUNK = 128
ROPE_BASE = 10000.0
EPS = 1e-6
IN_WIDTH = 4 * RET_WIDTH + 3 * CONV_WIDTH

Q0, K0, V0, G0, A0, B0, C0 = 0, 1024, 2048, 3072, 4096, 5120, 6144

LANES = 128
SUBLANES = 8
TM = 256
N_TILES = SEQ // TM
HIST = 32
TM_OUT = 512
CONV_RB = 128
STAGE_SLOTS = 4
STAGE_ROWS_IN = 32
STAGE_ROWS_SQ = 128
VMEM_LIMIT_MIX = 58 * 1024 * 1024
VMEM_LIMIT_OUT = 48 * 1024 * 1024

BF16 = jnp.bfloat16
F32 = jnp.float32


def _constant_tables():
    half = HALF
    inv_freq = ROPE_BASE ** (-np.arange(half, dtype=np.float64) / half)
    bases = np.concatenate([N_META + TM * np.arange(N_TILES, dtype=np.float64),
                            [-(CHUNK - N_META)]])
    ang_b = bases[:, None] * inv_freq[None, :]
    ang_o = np.arange(TM, dtype=np.float64)[:, None] * inv_freq[None, :]
    cos_b = np.broadcast_to(np.cos(ang_b)[:, None, :], (N_TILES + 1, 8, half))
    sin_b = np.broadcast_to(np.sin(ang_b)[:, None, :], (N_TILES + 1, 8, half))
    gamma = 1.0 - np.exp2(-5.0 - np.arange(RET_HEADS, dtype=np.float64))
    log_g = np.log(gamma)
    idx = np.arange(CHUNK, dtype=np.float64)
    rel = idx[:, None] - idx[None, :]
    scale = HEAD_DIM ** -0.5
    mask = np.where(rel[None] >= 0, np.exp(np.maximum(rel, 0.0)[None] * log_g[:, None, None]), 0.0)
    q_decay = np.exp((idx[None, :] + 1.0) * log_g[:, None])
    k_decay = np.exp((CHUNK - 1.0 - idx[None, :]) * log_g[:, None])
    chunk_decay = np.exp(CHUNK * log_g)
    qdec = np.broadcast_to(q_decay[:, :, None], (RET_HEADS, CHUNK, LANES))
    kdec = np.broadcast_to((k_decay * scale)[:, :, None], (RET_HEADS, CHUNK, LANES))
    f = lambda a: jnp.asarray(np.ascontiguousarray(a), dtype=F32)
    tabs = dict(cos_b=f(cos_b), sin_b=f(sin_b), cos_o=f(np.cos(ang_o)), sin_o=f(np.sin(ang_o)),
                mask=f(mask * scale), qdec=f(qdec), kdec=f(kdec))
    return tabs, tuple(float(c) for c in chunk_decay)


def _rms_norm_rows(xf, g):
    ms = jnp.mean(xf * xf, axis=-1, keepdims=True)
    return xf * lax.rsqrt(ms + EPS) * g


def _rotary(x, cos, sin):
    x1, x2 = x[:, :HALF], x[:, HALF:]
    return jnp.concatenate([x1 * cos - x2 * sin, x1 * sin + x2 * cos], axis=-1)


def _dot(a, b):
    return jnp.dot(a, b, preferred_element_type=F32)


def _dot_nt(a, b):
    return lax.dot_general(a, b, (((1,), (1,)), ((), ())), preferred_element_type=F32)


def _dot_tn(a, b):
    return lax.dot_general(a, b, (((0,), (0,)), ((), ())), preferred_element_type=F32)


def _load_cast_weight(w_hbm, w_vmem, stage_ref, sem_ref):
    n_slots, rows = stage_ref.shape[0], stage_ref.shape[1]
    n_chunks = w_hbm.shape[0] // rows

    def chunk_copy(c, slot):
        return pltpu.make_async_copy(w_hbm.at[pl.ds(c * rows, rows), :], stage_ref.at[slot],
                                     sem_ref.at[slot])

    for c in range(n_slots - 1):
        chunk_copy(c, c).start(priority=c % 2)

    def body(pair, carry):
        for parity in range(2):
            c = 2 * pair + parity
            slot = c % n_slots
            chunk_copy(c, slot).wait()
            nxt = c + n_slots - 1

            @pl.when(nxt < n_chunks)
            def _prefetch():
                chunk_copy(nxt, nxt % n_slots).start(priority=(parity + n_slots - 1) % 2)

            r0 = pl.multiple_of(c * rows, rows)
            w_vmem[pl.ds(r0, rows), :] = stage_ref[slot].astype(BF16)
        return carry

    lax.fori_loop(0, n_chunks // 2, body, 0)


def _mix_kernel(chunk_decay,
                x_ref, meta_ref, lng_ref, win_hbm, cosb_ref, sinb_ref, cosm_ref, sinm_ref,
                coso_ref, sino_ref, mask_ref, qdec_ref, kdec_ref, gng_ref,
                dww_ref, dwb_ref, clng_ref, clnb_ref, pww_hbm, pwb_ref,
                y_ref,
                win_ref, pww_ref, stage_in_ref, stage_pw_ref, sem_in_ref, sem_pw_ref,
                state_ref, uph_ref, conv_ref, hn_ref):
    i = pl.program_id(0)
    n_phase = uph_ref.shape[0]

    def state_update(h, k_rot, v):
        kd = (k_rot * jnp.concatenate([kdec_ref[h], kdec_ref[h]], axis=-1)).astype(BF16)
        state_ref[h] = state_ref[h] * chunk_decay[h] + _dot_tn(kd, v.astype(BF16))

    @pl.when(i == 0)
    def _meta():
        _load_cast_weight(win_hbm, win_ref, stage_in_ref, sem_in_ref)
        _load_cast_weight(pww_hbm, pww_ref, stage_pw_ref, sem_pw_ref)
        state_ref[...] = jnp.zeros_like(state_ref)
        hn_ref[0:CHUNK - N_META, :] = jnp.zeros((CHUNK - N_META, D_MODEL), BF16)
        hn_ref[CHUNK - N_META:CHUNK, :] = _rms_norm_rows(meta_ref[...], lng_ref[...]).astype(BF16)
        hm = hn_ref[0:CHUNK, :]
        cm, sm = cosm_ref[0][0:1, :], sinm_ref[0][0:1, :]
        cos = cm * coso_ref[:CHUNK] - sm * sino_ref[:CHUNK]
        sin = sm * coso_ref[:CHUNK] + cm * sino_ref[:CHUNK]
        k = _dot(hm, win_ref[:, K0:K0 + RET_WIDTH])
        v = _dot(hm, win_ref[:, V0:V0 + RET_WIDTH])
        for h in range(RET_HEADS):
            hs = slice(h * HEAD_DIM, (h + 1) * HEAD_DIM)
            state_update(h, _rotary(k[:, hs], cos, sin), v[:, hs])
        a = _dot(hm, win_ref[:, A0:A0 + CONV_WIDTH])
        b = _dot(hm, win_ref[:, B0:B0 + CONV_WIDTH])
        u = a * jax.nn.sigmoid(b)
        for p in range(n_phase):
            uph_ref[p, 0:HIST - p, :] = u[CHUNK - HIST + p:, :]

    hn_ref[...] = _rms_norm_rows(x_ref[...], lng_ref[...]).astype(BF16)
    hn = hn_ref[...]

    cb, sb = cosb_ref[0][0:1, :], sinb_ref[0][0:1, :]
    cos = cb * coso_ref[...] - sb * sino_ref[...]
    sin = sb * coso_ref[...] + cb * sino_ref[...]

    a = _dot(hn, win_ref[:, A0:A0 + CONV_WIDTH])
    b = _dot(hn, win_ref[:, B0:B0 + CONV_WIDTH])
    u = a * jax.nn.sigmoid(b)
    shift = HIST - (CONV_K - 1)
    for row in range(0, TM, CONV_RB):
        for p in range(n_phase):
            uph_ref[p, HIST - p:HIST - p + CONV_RB, :] = u[row:row + CONV_RB]
        for cbi in range(CONV_WIDTH // LANES):
            cs = slice(cbi * LANES, (cbi + 1) * LANES)
            acc = jnp.broadcast_to(dwb_ref[:, cs], (CONV_RB, LANES))
            for p in range(n_phase):
                taps = [j for j in range(CONV_K) if (j + shift) % n_phase == p]
                a0 = min((j + shift) // n_phase for j in taps)
                a1 = max((j + shift) // n_phase for j in taps)
                start = n_phase * a0
                win = uph_ref[p, start:start + CONV_RB + n_phase * (a1 - a0), cs]
                for j in taps:
                    o = n_phase * ((j + shift) // n_phase - a0)
                    acc = acc + win[o:o + CONV_RB] * dww_ref[j:j + 1, cs]
            conv_ref[row:row + CONV_RB, cs] = acc
        for p in range(n_phase):
            uph_ref[p, 0:HIST - p, :] = uph_ref[p, CONV_RB:CONV_RB + HIST - p, :]

    q = _dot(hn, win_ref[:, Q0:Q0 + RET_WIDTH])
    k = _dot(hn, win_ref[:, K0:K0 + RET_WIDTH])
    v = _dot(hn, win_ref[:, V0:V0 + RET_WIDTH])
    gr = _dot(hn, win_ref[:, G0:G0 + RET_WIDTH])
    for h in range(RET_HEADS):
        hs = slice(h * HEAD_DIM, (h + 1) * HEAD_DIM)
        q_rot = _rotary(q[:, hs], cos, sin)
        k_rot = _rotary(k[:, hs], cos, sin)
        qdec = jnp.concatenate([qdec_ref[h], qdec_ref[h]], axis=-1)
        for c in range(TM // CHUNK):
            rs = slice(c * CHUNK, (c + 1) * CHUNK)
            qc, kc, vc = q_rot[rs], k_rot[rs], v[rs, hs]
            scores = _dot_nt(qc.astype(BF16), kc.astype(BF16)) * mask_ref[h]
            inner = _dot(scores.astype(BF16), vc.astype(BF16))
            cross = _dot((qc * qdec).astype(BF16), state_ref[h].astype(BF16))
            state_update(h, kc, vc)
            y = inner + cross
            mu = jnp.mean(y, axis=-1, keepdims=True)
            var = jnp.mean(jnp.square(y - mu), axis=-1, keepdims=True)
            yn = (y - mu) * lax.rsqrt(var + EPS) * gng_ref[:, hs]
            y_ref[rs, hs] = (yn * jax.nn.silu(gr[rs, hs])).astype(y_ref.dtype)

    uf = conv_ref[...]
    mu = jnp.mean(uf, axis=-1, keepdims=True)
    var = jnp.mean(jnp.square(uf - mu), axis=-1, keepdims=True)
    uf = (uf - mu) * lax.rsqrt(var + EPS) * clng_ref[...] + clnb_ref[...]
    uc = _dot(jax.nn.silu(uf).astype(BF16), pww_ref[...]) + pwb_ref[...]
    gc = _dot(hn, win_ref[:, C0:C0 + CONV_WIDTH])
    y_ref[:, RET_WIDTH:] = (uc * jax.nn.silu(gc)).astype(y_ref.dtype)


def _out_kernel(y_ref, x_ref, wout_hbm, fg_ref, o_ref, wout_ref, stage_ref, sem_ref):
    @pl.when(pl.program_id(0) == 0)
    def _weights():
        _load_cast_weight(wout_hbm, wout_ref, stage_ref, sem_ref)

    h = x_ref[...] + _dot(y_ref[...], wout_ref[...])
    o_ref[...] = _rms_norm_rows(h, fg_ref[...])


def _resident(shape):
    nd = len(shape)
    return pl.BlockSpec(shape, lambda i: (0,) * nd, pipeline_mode=pl.Buffered(1))


def kernel(x, meta_tokens, ln_g, w_in, ret_gn_g, conv_dw_w, conv_dw_b, conv_ln_g, conv_ln_b,
           conv_pw_w, conv_pw_b, w_out, final_g):
    assert x.shape == (1, SEQ, D_MODEL) and w_in.shape == (1, D_MODEL, IN_WIDTH)
    tabs, chunk_decay = _constant_tables()
    x2 = x[0]
    row = lambda a: a.reshape(1, -1)
    hbm = pl.BlockSpec(memory_space=pl.ANY)

    mix_in = [
        (x2, pl.BlockSpec((TM, D_MODEL), lambda i: (i, 0))),
        (meta_tokens, _resident((N_META, D_MODEL))),
        (row(ln_g[0]), _resident((1, D_MODEL))),
        (w_in[0], hbm),
        (tabs["cos_b"], pl.BlockSpec((1, 8, HALF), lambda i: (i, 0, 0))),
        (tabs["sin_b"], pl.BlockSpec((1, 8, HALF), lambda i: (i, 0, 0))),
        (tabs["cos_b"], pl.BlockSpec((1, 8, HALF), lambda i: (N_TILES, 0, 0))),
        (tabs["sin_b"], pl.BlockSpec((1, 8, HALF), lambda i: (N_TILES, 0, 0))),
        (tabs["cos_o"], _resident((TM, HALF))),
        (tabs["sin_o"], _resident((TM, HALF))),
        (tabs["mask"], _resident((RET_HEADS, CHUNK, CHUNK))),
        (tabs["qdec"], _resident((RET_HEADS, CHUNK, LANES))),
        (tabs["kdec"], _resident((RET_HEADS, CHUNK, LANES))),
        (row(ret_gn_g[0]), _resident((1, RET_WIDTH))),
        (conv_dw_w[0], _resident((CONV_K, CONV_WIDTH))),
        (row(conv_dw_b[0]), _resident((1, CONV_WIDTH))),
        (row(conv_ln_g[0]), _resident((1, CONV_WIDTH))),
        (row(conv_ln_b[0]), _resident((1, CONV_WIDTH))),
        (conv_pw_w[0], hbm),
        (row(conv_pw_b[0]), _resident((1, CONV_WIDTH))),
    ]
    y = pl.pallas_call(
        functools.partial(_mix_kernel, chunk_decay),
        grid=(N_TILES,),
        in_specs=[s for _, s in mix_in],
        out_specs=pl.BlockSpec((TM, D_MODEL), lambda i: (i, 0)),
        out_shape=jax.ShapeDtypeStruct((SEQ, D_MODEL), BF16),
        scratch_shapes=[
            pltpu.VMEM((D_MODEL, IN_WIDTH), BF16),
            pltpu.VMEM((CONV_WIDTH, CONV_WIDTH), BF16),
            pltpu.VMEM((STAGE_SLOTS, STAGE_ROWS_IN, IN_WIDTH), F32),
            pltpu.VMEM((STAGE_SLOTS, STAGE_ROWS_SQ, CONV_WIDTH), F32),
            pltpu.SemaphoreType.DMA((STAGE_SLOTS,)),
            pltpu.SemaphoreType.DMA((STAGE_SLOTS,)),
            pltpu.VMEM((RET_HEADS, HEAD_DIM, HEAD_DIM), F32),
            pltpu.VMEM((SUBLANES, HIST + CONV_RB, CONV_WIDTH), F32),
            pltpu.VMEM((TM, CONV_WIDTH), F32),
            pltpu.VMEM((TM, D_MODEL), BF16),
        ],
        compiler_params=pltpu.CompilerParams(
            dimension_semantics=("arbitrary",), vmem_limit_bytes=VMEM_LIMIT_MIX),
        name="mix",
    )(*[a for a, _ in mix_in])

    out = pl.pallas_call(
        _out_kernel,
        grid=(SEQ // TM_OUT,),
        in_specs=[
            pl.BlockSpec((TM_OUT, D_MODEL), lambda i: (i, 0)),
            pl.BlockSpec((TM_OUT, D_MODEL), lambda i: (i, 0)),
            hbm,
            _resident((1, D_MODEL)),
        ],
        out_specs=pl.BlockSpec((TM_OUT, D_MODEL), lambda i: (i, 0)),
        out_shape=jax.ShapeDtypeStruct((SEQ, D_MODEL), x.dtype),
        scratch_shapes=[
            pltpu.VMEM((D_MODEL, D_MODEL), BF16),
            pltpu.VMEM((STAGE_SLOTS, STAGE_ROWS_SQ, D_MODEL), F32),
            pltpu.SemaphoreType.DMA((STAGE_SLOTS,)),
        ],
        compiler_params=pltpu.CompilerParams(
            dimension_semantics=("arbitrary",), vmem_limit_bytes=VMEM_LIMIT_OUT),
        name="out",
    )(y, x2, w_out[0], row(final_g))
    return out[None]
```

```python
import functools

import numpy as np
import jax
import jax.numpy as jnp
from jax import lax
from jax.experimental import pallas as pl
from jax.experimental.pallas import tpu as pltpu

D_MODEL = 2048
SEQ = 8192
N_META = 16
RET_WIDTH = 1024
RET_HEADS = 4
HEAD_DIM = 256
HALF = HEAD_DIM // 2
CONV_WIDTH = 1024
CONV_K = 31
CHUNK = 128
ROPE_BASE = 10000.0
EPS = 1e-6
IN_WIDTH = 4 * RET_WIDTH + 3 * CONV_WIDTH

Q0, K0, V0, G0, A0, B0, C0 = 0, 1024, 2048, 3072, 4096, 5120, 6144

LANES = 128
SUBLANES = 8
TM = 256
N_TILES = SEQ // TM
HIST = 32
TM_OUT = 512
CONV_RB = 128
STAGE_SLOTS = 8
STAGE_ROWS_IN = 16
STAGE_ROWS_SQ = 64
VMEM_LIMIT_MIX = 58 * 1024 * 1024
VMEM_LIMIT_OUT = 48 * 1024 * 1024

BF16 = jnp.bfloat16
F32 = jnp.float32


def _constant_tables():
    half = HALF
    inv_freq = ROPE_BASE ** (-np.arange(half, dtype=np.float64) / half)
    bases = np.concatenate([N_META + TM * np.arange(N_TILES, dtype=np.float64),
                            [-(CHUNK - N_META)]])
    ang_b = bases[:, None] * inv_freq[None, :]
    ang_o = np.arange(TM, dtype=np.float64)[:, None] * inv_freq[None, :]
    cos_b = np.broadcast_to(np.cos(ang_b)[:, None, :], (N_TILES + 1, 8, half))
    sin_b = np.broadcast_to(np.sin(ang_b)[:, None, :], (N_TILES + 1, 8, half))
    gamma = 1.0 - np.exp2(-5.0 - np.arange(RET_HEADS, dtype=np.float64))
    log_g = np.log(gamma)
    idx = np.arange(CHUNK, dtype=np.float64)
    rel = idx[:, None] - idx[None, :]
    scale = HEAD_DIM ** -0.5
    mask = np.where(rel[None] >= 0, np.exp(np.maximum(rel, 0.0)[None] * log_g[:, None, None]), 0.0)
    q_decay = np.exp((idx[None, :] + 1.0) * log_g[:, None])
    k_decay = np.exp((CHUNK - 1.0 - idx[None, :]) * log_g[:, None])
    chunk_decay = np.exp(CHUNK * log_g)
    qdec = np.broadcast_to(q_decay[:, :, None], (RET_HEADS, CHUNK, LANES))
    kdec = np.broadcast_to((k_decay * scale)[:, :, None], (RET_HEADS, CHUNK, LANES))
    f = lambda a: jnp.asarray(np.ascontiguousarray(a), dtype=F32)
    tabs = dict(cos_b=f(cos_b), sin_b=f(sin_b), cos_o=f(np.cos(ang_o)), sin_o=f(np.sin(ang_o)),
                mask=f(mask * scale), qdec=f(qdec), kdec=f(kdec))
    return tabs, tuple(float(c) for c in chunk_decay)


def _rms_norm_rows(xf, g):
    ms = jnp.mean(xf * xf, axis=-1, keepdims=True)
    return xf * lax.rsqrt(ms + EPS) * g


def _rotary(x, cos, sin):
    x1, x2 = x[:, :HALF], x[:, HALF:]
    return jnp.concatenate([x1 * cos - x2 * sin, x1 * sin + x2 * cos], axis=-1)


def _dot(a, b):
    return jnp.dot(a, b, preferred_element_type=F32)


def _dot_nt(a, b):
    return lax.dot_general(a, b, (((1,), (1,)), ((), ())), preferred_element_type=F32)


def _dot_tn(a, b):
    return lax.dot_general(a, b, (((0,), (0,)), ((), ())), preferred_element_type=F32)


def _load_cast_weight(w_hbm, w_vmem, stage_ref, sem_ref):
    n_slots, rows = stage_ref.shape[0], stage_ref.shape[1]
    n_chunks = w_hbm.shape[0] // rows

    def chunk_copy(c, slot):
        return pltpu.make_async_copy(w_hbm.at[pl.ds(c * rows, rows), :], stage_ref.at[slot],
                                     sem_ref.at[slot])

    for c in range(n_slots - 1):
        chunk_copy(c, c).start(priority=c % 2)

    def body(pair, carry):
        for parity in range(2):
            c = 2 * pair + parity
            slot = c % n_slots
            chunk_copy(c, slot).wait()
            nxt = c + n_slots - 1

            @pl.when(nxt < n_chunks)
            def _prefetch():
                chunk_copy(nxt, nxt % n_slots).start(priority=(parity + n_slots - 1) % 2)

            r0 = pl.multiple_of(c * rows, rows)
            w_vmem[pl.ds(r0, rows), :] = stage_ref[slot].astype(BF16)
        return carry

    lax.fori_loop(0, n_chunks // 2, body, 0)


def _mix_kernel(chunk_decay,
                x_ref, meta_ref, lng_ref, win_hbm, cosb_ref, sinb_ref, cosm_ref, sinm_ref,
                coso_ref, sino_ref, mask_ref, qdec_ref, kdec_ref, gng_ref,
                dww_ref, dwb_ref, clng_ref, clnb_ref, pww_hbm, pwb_ref,
                y_ref,
                win_ref, pww_ref, stage_in_ref, stage_pw_ref, sem_in_ref, sem_pw_ref,
                state_ref, uph_ref, conv_ref, hn_ref):
    i = pl.program_id(0)
    n_phase = uph_ref.shape[0]

    def state_update(h, k_rot, v):
        kd = (k_rot * jnp.concatenate([kdec_ref[h], kdec_ref[h]], axis=-1)).astype(BF16)
        state_ref[h] = state_ref[h] * chunk_decay[h] + _dot_tn(kd, v.astype(BF16))

    @pl.when(i == 0)
    def _meta():
        _load_cast_weight(win_hbm, win_ref, stage_in_ref, sem_in_ref)
        _load_cast_weight(pww_hbm, pww_ref, stage_pw_ref, sem_pw_ref)
        state_ref[...] = jnp.zeros_like(state_ref)
        hn_ref[0:CHUNK - N_META, :] = jnp.zeros((CHUNK - N_META, D_MODEL), BF16)
        hn_ref[CHUNK - N_META:CHUNK, :] = _rms_norm_rows(meta_ref[...], lng_ref[...]).astype(BF16)
        hm = hn_ref[0:CHUNK, :]
        cm, sm = cosm_ref[0][0:1, :], sinm_ref[0][0:1, :]
        cos = cm * coso_ref[:CHUNK] - sm * sino_ref[:CHUNK]
        sin = sm * coso_ref[:CHUNK] + cm * sino_ref[:CHUNK]
        k = _dot(hm, win_ref[:, K0:K0 + RET_WIDTH])
        v = _dot(hm, win_ref[:, V0:V0 + RET_WIDTH])
        for h in range(RET_HEADS):
            hs = slice(h * HEAD_DIM, (h + 1) * HEAD_DIM)
            state_update(h, _rotary(k[:, hs], cos, sin), v[:, hs])
        a = _dot(hm, win_ref[:, A0:A0 + CONV_WIDTH])
        b = _dot(hm, win_ref[:, B0:B0 + CONV_WIDTH])
        u = a * jax.nn.sigmoid(b)
        for p in range(n_phase):
            uph_ref[p, 0:HIST - p, :] = u[CHUNK - HIST + p:, :]

    hn_ref[...] = _rms_norm_rows(x_ref[...], lng_ref[...]).astype(BF16)
    hn = hn_ref[...]

    cb, sb = cosb_ref[0][0:1, :], sinb_ref[0][0:1, :]
    cos = cb * coso_ref[...] - sb * sino_ref[...]
    sin = sb * coso_ref[...] + cb * sino_ref[...]

    a = _dot(hn, win_ref[:, A0:A0 + CONV_WIDTH])
    b = _dot(hn, win_ref[:, B0:B0 + CONV_WIDTH])
    u = a * jax.nn.sigmoid(b)
    shift = HIST - (CONV_K - 1)
    for row in range(0, TM, CONV_RB):
        for p in range(n_phase):
            uph_ref[p, HIST - p:HIST - p + CONV_RB, :] = u[row:row + CONV_RB]
        for cbi in range(CONV_WIDTH // LANES):
            cs = slice(cbi * LANES, (cbi + 1) * LANES)
            acc = jnp.broadcast_to(dwb_ref[:, cs], (CONV_RB, LANES))
            for p in range(n_phase):
                taps = [j for j in range(CONV_K) if (j + shift) % n_phase == p]
                a0 = min((j + shift) // n_phase for j in taps)
                a1 = max((j + shift) // n_phase for j in taps)
                start = n_phase * a0
                win = uph_ref[p, start:start + CONV_RB + n_phase * (a1 - a0), cs]
                for j in taps:
                    o = n_phase * ((j + shift) // n_phase - a0)
                    acc = acc + win[o:o + CONV_RB] * dww_ref[j:j + 1, cs]
            conv_ref[row:row + CONV_RB, cs] = acc
        for p in range(n_phase):
            uph_ref[p, 0:HIST - p, :] = uph_ref[p, CONV_RB:CONV_RB + HIST - p, :]

    q = _dot(hn, win_ref[:, Q0:Q0 + RET_WIDTH])
    k = _dot(hn, win_ref[:, K0:K0 + RET_WIDTH])
    v = _dot(hn, win_ref[:, V0:V0 + RET_WIDTH])
    gr = _dot(hn, win_ref[:, G0:G0 + RET_WIDTH])
    for h in range(RET_HEADS):
        hs = slice(h * HEAD_DIM, (h + 1) * HEAD_DIM)
        q_rot = _rotary(q[:, hs], cos, sin)
        k_rot = _rotary(k[:, hs], cos, sin)
        qdec = jnp.concatenate([qdec_ref[h], qdec_ref[h]], axis=-1)
        for c in range(TM // CHUNK):
            rs = slice(c * CHUNK, (c + 1) * CHUNK)
            qc, kc, vc = q_rot[rs], k_rot[rs], v[rs, hs]
            scores = _dot_nt(qc.astype(BF16), kc.astype(BF16)) * mask_ref[h]
            inner = _dot(scores.astype(BF16), vc.astype(BF16))
            cross = _dot((qc * qdec).astype(BF16), state_ref[h].astype(BF16))
            state_update(h, kc, vc)
            y = inner + cross
            mu = jnp.mean(y, axis=-1, keepdims=True)
            var = jnp.mean(jnp.square(y - mu), axis=-1, keepdims=True)
            yn = (y - mu) * lax.rsqrt(var + EPS) * gng_ref[:, hs]
            y_ref[rs, hs] = (yn * jax.nn.silu(gr[rs, hs])).astype(y_ref.dtype)

    uf = conv_ref[...]
    mu = jnp.mean(uf, axis=-1, keepdims=True)
    var = jnp.mean(jnp.square(uf - mu), axis=-1, keepdims=True)
    uf = (uf - mu) * lax.rsqrt(var + EPS) * clng_ref[...] + clnb_ref[...]
    uc = _dot(jax.nn.silu(uf).astype(BF16), pww_ref[...]) + pwb_ref[...]
    gc = _dot(hn, win_ref[:, C0:C0 + CONV_WIDTH])
    y_ref[:, RET_WIDTH:] = (uc * jax.nn.silu(gc)).astype(y_ref.dtype)


def _out_kernel(y_ref, x_ref, wout_hbm, fg_ref, o_ref, wout_ref, stage_ref, sem_ref):
    @pl.when(pl.program_id(0) == 0)
    def _weights():
        _load_cast_weight(wout_hbm, wout_ref, stage_ref, sem_ref)

    h = x_ref[...] + _dot(y_ref[...], wout_ref[...])
    o_ref[...] = _rms_norm_rows(h, fg_ref[...])


def _resident(shape):
    nd = len(shape)
    return pl.BlockSpec(shape, lambda i: (0,) * nd, pipeline_mode=pl.Buffered(1))


def kernel(x, meta_tokens, ln_g, w_in, ret_gn_g, conv_dw_w, conv_dw_b, conv_ln_g, conv_ln_b,
           conv_pw_w, conv_pw_b, w_out, final_g):
    assert x.shape == (1, SEQ, D_MODEL) and w_in.shape == (1, D_MODEL, IN_WIDTH)
    tabs, chunk_decay = _constant_tables()
    x2 = x[0]
    row = lambda a: a.reshape(1, -1)
    hbm = pl.BlockSpec(memory_space=pl.ANY)

    mix_in = [
        (x2, pl.BlockSpec((TM, D_MODEL), lambda i: (i, 0))),
        (meta_tokens, _resident((N_META, D_MODEL))),
        (row(ln_g[0]), _resident((1, D_MODEL))),
        (w_in[0], hbm),
        (tabs["cos_b"], pl.BlockSpec((1, 8, HALF), lambda i: (i, 0, 0))),
        (tabs["sin_b"], pl.BlockSpec((1, 8, HALF), lambda i: (i, 0, 0))),
        (tabs["cos_b"], pl.BlockSpec((1, 8, HALF), lambda i: (N_TILES, 0, 0))),
        (tabs["sin_b"], pl.BlockSpec((1, 8, HALF), lambda i: (N_TILES, 0, 0))),
        (tabs["cos_o"], _resident((TM, HALF))),
        (tabs["sin_o"], _resident((TM, HALF))),
        (tabs["mask"], _resident((RET_HEADS, CHUNK, CHUNK))),
        (tabs["qdec"], _resident((RET_HEADS, CHUNK, LANES))),
        (tabs["kdec"], _resident((RET_HEADS, CHUNK, LANES))),
        (row(ret_gn_g[0]), _resident((1, RET_WIDTH))),
        (conv_dw_w[0], _resident((CONV_K, CONV_WIDTH))),
        (row(conv_dw_b[0]), _resident((1, CONV_WIDTH))),
        (row(conv_ln_g[0]), _resident((1, CONV_WIDTH))),
        (row(conv_ln_b[0]), _resident((1, CONV_WIDTH))),
        (conv_pw_w[0], hbm),
        (row(conv_pw_b[0]), _resident((1, CONV_WIDTH))),
    ]
    y = pl.pallas_call(
        functools.partial(_mix_kernel, chunk_decay),
        grid=(N_TILES,),
        in_specs=[s for _, s in mix_in],
        out_specs=pl.BlockSpec((TM, D_MODEL), lambda i: (i, 0)),
        out_shape=jax.ShapeDtypeStruct((SEQ, D_MODEL), BF16),
        scratch_shapes=[
            pltpu.VMEM((D_MODEL, IN_WIDTH), BF16),
            pltpu.VMEM((CONV_WIDTH, CONV_WIDTH), BF16),
            pltpu.VMEM((STAGE_SLOTS, STAGE_ROWS_IN, IN_WIDTH), F32),
            pltpu.VMEM((STAGE_SLOTS, STAGE_ROWS_SQ, CONV_WIDTH), F32),
            pltpu.SemaphoreType.DMA((STAGE_SLOTS,)),
            pltpu.SemaphoreType.DMA((STAGE_SLOTS,)),
            pltpu.VMEM((RET_HEADS, HEAD_DIM, HEAD_DIM), F32),
            pltpu.VMEM((SUBLANES, HIST + CONV_RB, CONV_WIDTH), F32),
            pltpu.VMEM((TM, CONV_WIDTH), F32),
            pltpu.VMEM((TM, D_MODEL), BF16),
        ],
        compiler_params=pltpu.CompilerParams(
            dimension_semantics=("arbitrary",), vmem_limit_bytes=VMEM_LIMIT_MIX),
        name="mix",
    )(*[a for a, _ in mix_in])

    out = pl.pallas_call(
        _out_kernel,
        grid=(SEQ // TM_OUT,),
        in_specs=[
            pl.BlockSpec((TM_OUT, D_MODEL), lambda i: (i, 0)),
            pl.BlockSpec((TM_OUT, D_MODEL), lambda i: (i, 0)),
            hbm,
            _resident((1, D_MODEL)),
        ],
        out_specs=pl.BlockSpec((TM_OUT, D_MODEL), lambda i: (i, 0)),
        out_shape=jax.ShapeDtypeStruct((SEQ, D_MODEL), x.dtype),
        scratch_shapes=[
            pltpu.VMEM((D_MODEL, D_MODEL), BF16),
            pltpu.VMEM((STAGE_SLOTS, STAGE_ROWS_SQ, D_MODEL), F32),
            pltpu.SemaphoreType.DMA((STAGE_SLOTS,)),
        ],
        compiler_params=pltpu.CompilerParams(
            dimension_semantics=("arbitrary",), vmem_limit_bytes=VMEM_LIMIT_OUT),
        name="out",
    )(y, x2, w_out[0], row(final_g))
    return out[None]
```

```python
import functools

import numpy as np
import jax
import jax.numpy as jnp
from jax import lax
from jax.experimental import pallas as pl
from jax.experimental.pallas import tpu as pltpu

D_MODEL = 2048
SEQ = 8192
N_META = 16
RET_WIDTH = 1024
RET_HEADS = 4
HEAD_DIM = 256
HALF = HEAD_DIM // 2
CONV_WIDTH = 1024
CONV_K = 31
CHUNK = 128
ROPE_BASE = 10000.0
EPS = 1e-6
IN_WIDTH = 4 * RET_WIDTH + 3 * CONV_WIDTH

Q0, K0, V0, G0, A0, B0, C0 = 0, 1024, 2048, 3072, 4096, 5120, 6144

LANES = 128
SUBLANES = 8
TM = 256
N_TILES = SEQ // TM
HIST = 32
TM_OUT = 512
CONV_RB = 128
CONV_BLOCKS_PER_ITER = 4
STAGE_ROWS = 32
STAGE_ROWS_OUT = 64
STAGE_SLOTS_OUT = 8
VMEM_LIMIT_MIX = 58 * 1024 * 1024
VMEM_LIMIT_OUT = 48 * 1024 * 1024

BF16 = jnp.bfloat16
F32 = jnp.float32


def _constant_tables():
    half = HALF
    inv_freq = ROPE_BASE ** (-np.arange(half, dtype=np.float64) / half)
    bases = np.concatenate([N_META + TM * np.arange(N_TILES, dtype=np.float64),
                            [-(CHUNK - N_META)]])
    ang_b = bases[:, None] * inv_freq[None, :]
    ang_o = np.arange(TM, dtype=np.float64)[:, None] * inv_freq[None, :]
    cos_b = np.broadcast_to(np.cos(ang_b)[:, None, :], (N_TILES + 1, 8, half))
    sin_b = np.broadcast_to(np.sin(ang_b)[:, None, :], (N_TILES + 1, 8, half))
    gamma = 1.0 - np.exp2(-5.0 - np.arange(RET_HEADS, dtype=np.float64))
    log_g = np.log(gamma)
    idx = np.arange(CHUNK, dtype=np.float64)
    rel = idx[:, None] - idx[None, :]
    scale = HEAD_DIM ** -0.5
    mask = np.where(rel[None] >= 0, np.exp(np.maximum(rel, 0.0)[None] * log_g[:, None, None]), 0.0)
    q_decay = np.exp((idx[None, :] + 1.0) * log_g[:, None])
    k_decay = np.exp((CHUNK - 1.0 - idx[None, :]) * log_g[:, None])
    chunk_decay = np.exp(CHUNK * log_g)
    qdec = np.broadcast_to(q_decay[:, :, None], (RET_HEADS, CHUNK, LANES))
    kdec = np.broadcast_to((k_decay * scale)[:, :, None], (RET_HEADS, CHUNK, LANES))
    f = lambda a: jnp.asarray(np.ascontiguousarray(a), dtype=F32)
    tabs = dict(cos_b=f(cos_b), sin_b=f(sin_b), cos_o=f(np.cos(ang_o)), sin_o=f(np.sin(ang_o)),
                mask=f(mask * scale), qdec=f(qdec), kdec=f(kdec))
    return tabs, tuple(float(c) for c in chunk_decay)


def _rms_norm_rows(xf, g):
    ms = jnp.mean(xf * xf, axis=-1, keepdims=True)
    return xf * lax.rsqrt(ms + EPS) * g


def _rotary(x, cos, sin):
    x1, x2 = x[:, :HALF], x[:, HALF:]
    return jnp.concatenate([x1 * cos - x2 * sin, x1 * sin + x2 * cos], axis=-1)


def _dot(a, b):
    return jnp.dot(a, b, preferred_element_type=F32)


def _dot_nt(a, b):
    return lax.dot_general(a, b, (((1,), (1,)), ((), ())), preferred_element_type=F32)


def _dot_tn(a, b):
    return lax.dot_general(a, b, (((0,), (0,)), ((), ())), preferred_element_type=F32)


def _load_cast_weight(w_hbm, w_vmem, stage_ref, sem_ref, rows):
    n_slots = stage_ref.shape[0] // rows
    n_chunks = w_hbm.shape[0] // rows
    assert sem_ref.shape[0] >= n_slots and n_chunks >= n_slots

    def slot_rows(slot):
        return pl.ds(pl.multiple_of(slot * rows, rows), rows)

    def chunk_copy(c, slot):
        return pltpu.make_async_copy(w_hbm.at[pl.ds(c * rows, rows), :],
                                     stage_ref.at[slot_rows(slot), :], sem_ref.at[slot])

    for c in range(n_slots - 1):
        chunk_copy(c, c).start()

    def body(c, carry):
        slot = c % n_slots
        chunk_copy(c, slot).wait()
        nxt = c + n_slots - 1

        @pl.when(nxt < n_chunks)
        def _prefetch():
            chunk_copy(nxt, nxt % n_slots).start()

        r0 = pl.multiple_of(c * rows, rows)
        w_vmem[pl.ds(r0, rows), :] = stage_ref[slot_rows(slot), :].astype(BF16)
        return carry

    lax.fori_loop(0, n_chunks, body, 0)


def _mix_kernel(chunk_decay,
                x_ref, meta_ref, lng_ref, win_hbm, cosb_ref, sinb_ref, cosm_ref, sinm_ref,
                coso_ref, sino_ref, mask_ref, qdec_ref, kdec_ref, gng_ref,
                dww_ref, dwb_ref, clng_ref, clnb_ref, pww_hbm, pwb_ref,
                y_ref,
                win_ref, pww_ref, sem_ref,
                state_ref, uext_ref, uph_ref, proj_ref, conv_ref, hn_ref):
    i = pl.program_id(0)
    n_phase = SUBLANES

    def state_update(h, k_rot, v):
        kd = (k_rot * jnp.concatenate([kdec_ref[h], kdec_ref[h]], axis=-1)).astype(BF16)
        state_ref[h] = state_ref[h] * chunk_decay[h] + _dot_tn(kd, v.astype(BF16))

    @pl.when(i == 0)
    def _meta():
        stage_w = proj_ref.shape[1]
        for c0 in range(0, IN_WIDTH, stage_w):
            cols = pl.ds(c0, min(stage_w, IN_WIDTH - c0))
            _load_cast_weight(win_hbm.at[:, cols], win_ref.at[:, cols],
                              proj_ref.at[:, pl.ds(0, cols.size)], sem_ref, STAGE_ROWS)
        _load_cast_weight(pww_hbm, pww_ref, proj_ref.at[:, pl.ds(0, CONV_WIDTH)], sem_ref,
                          STAGE_ROWS)
        state_ref[...] = jnp.zeros_like(state_ref)
        hn_ref[0:CHUNK - N_META, :] = jnp.zeros((CHUNK - N_META, D_MODEL), BF16)
        hn_ref[CHUNK - N_META:CHUNK, :] = _rms_norm_rows(meta_ref[...], lng_ref[...]).astype(BF16)
        hm = hn_ref[0:CHUNK, :]
        cm, sm = cosm_ref[0][0:1, :], sinm_ref[0][0:1, :]
        cos = cm * coso_ref[:CHUNK] - sm * sino_ref[:CHUNK]
        sin = sm * coso_ref[:CHUNK] + cm * sino_ref[:CHUNK]
        k = _dot(hm, win_ref[:, K0:K0 + RET_WIDTH])
        v = _dot(hm, win_ref[:, V0:V0 + RET_WIDTH])
        for h in range(RET_HEADS):
            hs = slice(h * HEAD_DIM, (h + 1) * HEAD_DIM)
            state_update(h, _rotary(k[:, hs], cos, sin), v[:, hs])
        a = _dot(hm, win_ref[:, A0:A0 + CONV_WIDTH])
        b = _dot(hm, win_ref[:, B0:B0 + CONV_WIDTH])
        u = a * jax.nn.sigmoid(b)
        uext_ref[0:HIST, :] = u[CHUNK - HIST:, :]

    hn_ref[...] = _rms_norm_rows(x_ref[...], lng_ref[...]).astype(BF16)
    hn = hn_ref[...]

    cb, sb = cosb_ref[0][0:1, :], sinb_ref[0][0:1, :]
    cos = cb * coso_ref[...] - sb * sino_ref[...]
    sin = sb * coso_ref[...] + cb * sino_ref[...]

    a = _dot(hn, win_ref[:, A0:A0 + CONV_WIDTH])
    b = _dot(hn, win_ref[:, B0:B0 + CONV_WIDTH])
    uext_ref[HIST:HIST + TM, :] = a * jax.nn.sigmoid(b)

    shift = HIST - (CONV_K - 1)
    n_acc = CONV_RB // SUBLANES
    n_iter = CONV_WIDTH // LANES // CONV_BLOCKS_PER_ITER
    proj_slice = 4 * RET_WIDTH // n_iter

    def conv_and_project(t, carry):
        pcol = pl.multiple_of(t * proj_slice, proj_slice)
        proj_ref[:, pl.ds(pcol, proj_slice)] = _dot(hn_ref[...],
                                                    win_ref[:, pl.ds(Q0 + pcol, proj_slice)])
        for blk in range(CONV_BLOCKS_PER_ITER):
            conv_lane_block(pl.multiple_of((t * CONV_BLOCKS_PER_ITER + blk) * LANES, LANES), blk)
        return carry

    def conv_lane_block(col, blk):
        cs = pl.ds(col, LANES)
        window = uext_ref[:, cs]
        for p in range(1, n_phase):
            uph_ref[blk, p, SUBLANES - p:SUBLANES - p + HIST + TM, :] = window
        bias = jnp.broadcast_to(dwb_ref[:, cs], (SUBLANES, LANES))
        for row in range(0, TM, CONV_RB):
            acc = [bias] * n_acc
            for p in range(n_phase):
                taps = [j for j in range(CONV_K) if (j + shift) % n_phase == p]
                a0 = min((j + shift) // n_phase for j in taps)
                a1 = max((j + shift) // n_phase for j in taps)
                wts = [(jnp.broadcast_to(dww_ref[j:j + 1, cs], (SUBLANES, LANES)),
                        (j + shift) // n_phase - a0) for j in taps]
                for kk in range(n_acc + a1 - a0):
                    r = row + SUBLANES * (a0 + kk)
                    if p == 0:
                        wv = uext_ref[r:r + SUBLANES, cs]
                    else:
                        wv = uph_ref[blk, p, SUBLANES + r:2 * SUBLANES + r, :]
                    for wb, da in wts:
                        if 0 <= kk - da < n_acc:
                            acc[kk - da] = acc[kk - da] + wv * wb
            conv_ref[row:row + CONV_RB, cs] = jnp.concatenate(acc, axis=0)

    lax.fori_loop(0, n_iter, conv_and_project, 0)
    uext_ref[0:HIST, :] = uext_ref[TM:TM + HIST, :]

    q = proj_ref[:, Q0:Q0 + RET_WIDTH]
    k = proj_ref[:, K0:K0 + RET_WIDTH]
    v = proj_ref[:, V0:V0 + RET_WIDTH]
    gr = proj_ref[:, G0:G0 + RET_WIDTH]
    for h in range(RET_HEADS):
        hs = slice(h * HEAD_DIM, (h + 1) * HEAD_DIM)
        q_rot = _rotary(q[:, hs], cos, sin)
        k_rot = _rotary(k[:, hs], cos, sin)
        qdec = jnp.concatenate([qdec_ref[h], qdec_ref[h]], axis=-1)
        for c in range(TM // CHUNK):
            rs = slice(c * CHUNK, (c + 1) * CHUNK)
            qc, kc, vc = q_rot[rs], k_rot[rs], v[rs, hs]
            scores = _dot_nt(qc.astype(BF16), kc.astype(BF16)) * mask_ref[h]
            inner = _dot(scores.astype(BF16), vc.astype(BF16))
            cross = _dot((qc * qdec).astype(BF16), state_ref[h].astype(BF16))
            state_update(h, kc, vc)
            y = inner + cross
            mu = jnp.mean(y, axis=-1, keepdims=True)
            var = jnp.mean(jnp.square(y - mu), axis=-1, keepdims=True)
            yn = (y - mu) * lax.rsqrt(var + EPS) * gng_ref[:, hs]
            y_ref[rs, hs] = (yn * jax.nn.silu(gr[rs, hs])).astype(y_ref.dtype)

    uf = conv_ref[...]
    mu = jnp.mean(uf, axis=-1, keepdims=True)
    var = jnp.mean(jnp.square(uf - mu), axis=-1, keepdims=True)
    uf = (uf - mu) * lax.rsqrt(var + EPS) * clng_ref[...] + clnb_ref[...]
    uc = _dot(jax.nn.silu(uf).astype(BF16), pww_ref[...]) + pwb_ref[...]
    gc = _dot(hn, win_ref[:, C0:C0 + CONV_WIDTH])
    y_ref[:, RET_WIDTH:] = (uc * jax.nn.silu(gc)).astype(y_ref.dtype)


def _out_kernel(y_ref, x_ref, wout_hbm, fg_ref, o_ref, wout_ref, stage_ref, sem_ref):
    @pl.when(pl.program_id(0) == 0)
    def _weights():
        _load_cast_weight(wout_hbm, wout_ref, stage_ref, sem_ref, STAGE_ROWS_OUT)

    h = x_ref[...] + _dot(y_ref[...], wout_ref[...])
    o_ref[...] = _rms_norm_rows(h, fg_ref[...])


def _resident(shape):
    nd = len(shape)
    return pl.BlockSpec(shape, lambda i: (0,) * nd, pipeline_mode=pl.Buffered(1))


def kernel(x, meta_tokens, ln_g, w_in, ret_gn_g, conv_dw_w, conv_dw_b, conv_ln_g, conv_ln_b,
           conv_pw_w, conv_pw_b, w_out, final_g):
    assert x.shape == (1, SEQ, D_MODEL) and w_in.shape == (1, D_MODEL, IN_WIDTH)
    tabs, chunk_decay = _constant_tables()
    x2 = x[0]
    row = lambda a: a.reshape(1, -1)
    hbm = pl.BlockSpec(memory_space=pl.ANY)

    mix_in = [
        (x2, pl.BlockSpec((TM, D_MODEL), lambda i: (i, 0))),
        (meta_tokens, _resident((N_META, D_MODEL))),
        (row(ln_g[0]), _resident((1, D_MODEL))),
        (w_in[0], hbm),
        (tabs["cos_b"], pl.BlockSpec((1, 8, HALF), lambda i: (i, 0, 0))),
        (tabs["sin_b"], pl.BlockSpec((1, 8, HALF), lambda i: (i, 0, 0))),
        (tabs["cos_b"], pl.BlockSpec((1, 8, HALF), lambda i: (N_TILES, 0, 0))),
        (tabs["sin_b"], pl.BlockSpec((1, 8, HALF), lambda i: (N_TILES, 0, 0))),
        (tabs["cos_o"], _resident((TM, HALF))),
        (tabs["sin_o"], _resident((TM, HALF))),
        (tabs["mask"], _resident((RET_HEADS, CHUNK, CHUNK))),
        (tabs["qdec"], _resident((RET_HEADS, CHUNK, LANES))),
        (tabs["kdec"], _resident((RET_HEADS, CHUNK, LANES))),
        (row(ret_gn_g[0]), _resident((1, RET_WIDTH))),
        (conv_dw_w[0], _resident((CONV_K, CONV_WIDTH))),
        (row(conv_dw_b[0]), _resident((1, CONV_WIDTH))),
        (row(conv_ln_g[0]), _resident((1, CONV_WIDTH))),
        (row(conv_ln_b[0]), _resident((1, CONV_WIDTH))),
        (conv_pw_w[0], hbm),
        (row(conv_pw_b[0]), _resident((1, CONV_WIDTH))),
    ]
    y = pl.pallas_call(
        functools.partial(_mix_kernel, chunk_decay),
        grid=(N_TILES,),
        in_specs=[s for _, s in mix_in],
        out_specs=pl.BlockSpec((TM, D_MODEL), lambda i: (i, 0)),
        out_shape=jax.ShapeDtypeStruct((SEQ, D_MODEL), BF16),
        scratch_shapes=[
            pltpu.VMEM((D_MODEL, IN_WIDTH), BF16),
            pltpu.VMEM((CONV_WIDTH, CONV_WIDTH), BF16),
            pltpu.SemaphoreType.DMA((TM // STAGE_ROWS,)),
            pltpu.VMEM((RET_HEADS, HEAD_DIM, HEAD_DIM), F32),
            pltpu.VMEM((HIST + TM, CONV_WIDTH), F32),
            pltpu.VMEM((CONV_BLOCKS_PER_ITER, SUBLANES, SUBLANES + HIST + TM, LANES), F32),
            pltpu.VMEM((TM, 4 * RET_WIDTH), F32),
            pltpu.VMEM((TM, CONV_WIDTH), F32),
            pltpu.VMEM((TM, D_MODEL), BF16),
        ],
        compiler_params=pltpu.CompilerParams(
            dimension_semantics=("arbitrary",), vmem_limit_bytes=VMEM_LIMIT_MIX),
        name="mix",
    )(*[a for a, _ in mix_in])

    out = pl.pallas_call(
        _out_kernel,
        grid=(SEQ // TM_OUT,),
        in_specs=[
            pl.BlockSpec((TM_OUT, D_MODEL), lambda i: (i, 0)),
            pl.BlockSpec((TM_OUT, D_MODEL), lambda i: (i, 0)),
            hbm,
            _resident((1, D_MODEL)),
        ],
        out_specs=pl.BlockSpec((TM_OUT, D_MODEL), lambda i: (i, 0)),
        out_shape=jax.ShapeDtypeStruct((SEQ, D_MODEL), x.dtype),
        scratch_shapes=[
            pltpu.VMEM((D_MODEL, D_MODEL), BF16),
            pltpu.VMEM((STAGE_SLOTS_OUT * STAGE_ROWS_OUT, D_MODEL), F32),
            pltpu.SemaphoreType.DMA((STAGE_SLOTS_OUT,)),
        ],
        compiler_params=pltpu.CompilerParams(
            dimension_semantics=("arbitrary",), vmem_limit_bytes=VMEM_LIMIT_OUT),
        name="out",
    )(y, x2, w_out[0], row(final_g))
    return out[None]
```

```python
import functools

import numpy as np
import jax
import jax.numpy as jnp
from jax import lax
from jax.experimental import pallas as pl
from jax.experimental.pallas import tpu as pltpu

D_MODEL = 2048
SEQ = 8192
N_META = 16
RET_WIDTH = 1024
RET_HEADS = 4
HEAD_DIM = 256
HALF = HEAD_DIM // 2
CONV_WIDTH = 1024
CONV_K = 31
CHUNK = 128
ROPE_BASE = 10000.0
EPS = 1e-6
IN_WIDTH = 4 * RET_WIDTH + 3 * CONV_WIDTH

Q0, K0, V0, G0, A0, B0, C0 = 0, 1024, 2048, 3072, 4096, 5120, 6144

LANES = 128
SUBLANES = 8
TM = 256
N_TILES = SEQ // TM
HIST = 32
TM_OUT = 512
CONV_RB = 128
CONV_CHAINS = 2
STAGE_ROWS = 32
STAGE_ROWS_OUT = 64
STAGE_SLOTS_OUT = 8
VMEM_LIMIT_MIX = 58 * 1024 * 1024
VMEM_LIMIT_OUT = 48 * 1024 * 1024

BF16 = jnp.bfloat16
F32 = jnp.float32


def _constant_tables():
    half = HALF
    inv_freq = ROPE_BASE ** (-np.arange(half, dtype=np.float64) / half)
    bases = np.concatenate([N_META + TM * np.arange(N_TILES, dtype=np.float64),
                            [-(CHUNK - N_META)]])
    ang_b = bases[:, None] * inv_freq[None, :]
    ang_o = np.arange(TM, dtype=np.float64)[:, None] * inv_freq[None, :]
    cos_b = np.broadcast_to(np.cos(ang_b)[:, None, :], (N_TILES + 1, 8, half))
    sin_b = np.broadcast_to(np.sin(ang_b)[:, None, :], (N_TILES + 1, 8, half))
    gamma = 1.0 - np.exp2(-5.0 - np.arange(RET_HEADS, dtype=np.float64))
    log_g = np.log(gamma)
    idx = np.arange(CHUNK, dtype=np.float64)
    rel = idx[:, None] - idx[None, :]
    scale = HEAD_DIM ** -0.5
    mask = np.where(rel[None] >= 0, np.exp(np.maximum(rel, 0.0)[None] * log_g[:, None, None]), 0.0)
    q_decay = np.exp((idx[None, :] + 1.0) * log_g[:, None])
    k_decay = np.exp((CHUNK - 1.0 - idx[None, :]) * log_g[:, None])
    chunk_decay = np.exp(CHUNK * log_g)
    qdec = np.broadcast_to(q_decay[:, :, None], (RET_HEADS, CHUNK, LANES))
    kdec = np.broadcast_to((k_decay * scale)[:, :, None], (RET_HEADS, CHUNK, LANES))
    f = lambda a: jnp.asarray(np.ascontiguousarray(a), dtype=F32)
    tabs = dict(cos_b=f(cos_b), sin_b=f(sin_b), cos_o=f(np.cos(ang_o)), sin_o=f(np.sin(ang_o)),
                mask=f(mask * scale), qdec=f(qdec), kdec=f(kdec))
    return tabs, tuple(float(c) for c in chunk_decay)


def _rms_norm_rows(xf, g):
    ms = jnp.mean(xf * xf, axis=-1, keepdims=True)
    return xf * lax.rsqrt(ms + EPS) * g


def _rotary(x, cos, sin):
    x1, x2 = x[:, :HALF], x[:, HALF:]
    return jnp.concatenate([x1 * cos - x2 * sin, x1 * sin + x2 * cos], axis=-1)


def _ordered_zero(v):
    bits = lax.bitcast_convert_type(v, jnp.uint32)
    return lax.bitcast_convert_type((bits >> 16) >> 16, F32)


def _dot(a, b):
    return jnp.dot(a, b, preferred_element_type=F32)


def _dot_nt(a, b):
    return lax.dot_general(a, b, (((1,), (1,)), ((), ())), preferred_element_type=F32)


def _dot_tn(a, b):
    return lax.dot_general(a, b, (((0,), (0,)), ((), ())), preferred_element_type=F32)


def _load_cast_weight(w_hbm, w_vmem, stage_ref, sem_ref, rows):
    n_slots = stage_ref.shape[0] // rows
    n_chunks = w_hbm.shape[0] // rows
    assert sem_ref.shape[0] >= n_slots and n_chunks >= n_slots

    def slot_rows(slot):
        return pl.ds(pl.multiple_of(slot * rows, rows), rows)

    def chunk_copy(c, slot):
        return pltpu.make_async_copy(w_hbm.at[pl.ds(c * rows, rows), :],
                                     stage_ref.at[slot_rows(slot), :], sem_ref.at[slot])

    for c in range(n_slots - 1):
        chunk_copy(c, c).start()

    def body(c, carry):
        slot = c % n_slots
        chunk_copy(c, slot).wait()
        nxt = c + n_slots - 1

        @pl.when(nxt < n_chunks)
        def _prefetch():
            chunk_copy(nxt, nxt % n_slots).start()

        r0 = pl.multiple_of(c * rows, rows)
        w_vmem[pl.ds(r0, rows), :] = stage_ref[slot_rows(slot), :].astype(BF16)
        return carry

    lax.fori_loop(0, n_chunks, body, 0)


def _mix_kernel(chunk_decay,
                x_ref, meta_ref, lng_ref, win_hbm, cosb_ref, sinb_ref, cosm_ref, sinm_ref,
                coso_ref, sino_ref, mask_ref, qdec_ref, kdec_ref, gng_ref,
                dww_ref, dwb_ref, clng_ref, clnb_ref, pww_hbm, pwb_ref,
                y_ref,
                win_ref, pww_ref, sem_ref,
                state_ref, uext_ref, uph_ref, proj_ref, conv_ref, hn_ref):
    i = pl.program_id(0)
    n_phase = SUBLANES

    def state_update(h, k_rot, v):
        kd = (k_rot * jnp.concatenate([kdec_ref[h], kdec_ref[h]], axis=-1)).astype(BF16)
        state_ref[h] = state_ref[h] * chunk_decay[h] + _dot_tn(kd, v.astype(BF16))

    @pl.when(i == 0)
    def _meta():
        stage_w = proj_ref.shape[1]
        for c0 in range(0, IN_WIDTH, stage_w):
            cols = pl.ds(c0, min(stage_w, IN_WIDTH - c0))
            _load_cast_weight(win_hbm.at[:, cols], win_ref.at[:, cols],
                              proj_ref.at[:, pl.ds(0, cols.size)], sem_ref, STAGE_ROWS)
        _load_cast_weight(pww_hbm, pww_ref, proj_ref.at[:, pl.ds(0, CONV_WIDTH)], sem_ref,
                          STAGE_ROWS)
        state_ref[...] = jnp.zeros_like(state_ref)
        hn_ref[0:CHUNK - N_META, :] = jnp.zeros((CHUNK - N_META, D_MODEL), BF16)
        hn_ref[CHUNK - N_META:CHUNK, :] = _rms_norm_rows(meta_ref[...], lng_ref[...]).astype(BF16)
        hm = hn_ref[0:CHUNK, :]
        cm, sm = cosm_ref[0][0:1, :], sinm_ref[0][0:1, :]
        cos = cm * coso_ref[:CHUNK] - sm * sino_ref[:CHUNK]
        sin = sm * coso_ref[:CHUNK] + cm * sino_ref[:CHUNK]
        k = _dot(hm, win_ref[:, K0:K0 + RET_WIDTH])
        v = _dot(hm, win_ref[:, V0:V0 + RET_WIDTH])
        for h in range(RET_HEADS):
            hs = slice(h * HEAD_DIM, (h + 1) * HEAD_DIM)
            state_update(h, _rotary(k[:, hs], cos, sin), v[:, hs])
        a = _dot(hm, win_ref[:, A0:A0 + CONV_WIDTH])
        b = _dot(hm, win_ref[:, B0:B0 + CONV_WIDTH])
        u = a * jax.nn.sigmoid(b)
        uext_ref[0:HIST, :] = u[CHUNK - HIST:, :]

    hn_ref[...] = _rms_norm_rows(x_ref[...], lng_ref[...]).astype(BF16)
    hn = hn_ref[...]

    cb, sb = cosb_ref[0][0:1, :], sinb_ref[0][0:1, :]
    cos = cb * coso_ref[...] - sb * sino_ref[...]
    sin = sb * coso_ref[...] + cb * sino_ref[...]

    a = _dot(hn, win_ref[:, A0:A0 + CONV_WIDTH])
    b = _dot(hn, win_ref[:, B0:B0 + CONV_WIDTH])
    uext_ref[HIST:HIST + TM, :] = a * jax.nn.sigmoid(b)

    shift = HIST - (CONV_K - 1)
    n_acc = CONV_RB // SUBLANES
    n_iter = TM // CONV_RB
    proj_slice = 4 * RET_WIDTH // n_iter

    def conv_and_project(t, carry):
        pcol = pl.multiple_of(t * proj_slice, proj_slice)
        proj_ref[:, pl.ds(pcol, proj_slice)] = _dot(hn_ref[...],
                                                    win_ref[:, pl.ds(Q0 + pcol, proj_slice)])
        row = pl.multiple_of(t * CONV_RB, CONV_RB)
        window = uext_ref[pl.ds(row, HIST + CONV_RB), :]
        for p in range(1, n_phase):
            uph_ref[p, SUBLANES - p:SUBLANES - p + HIST + CONV_RB, :] = window

        def window_vreg(p, idx, cs):
            if p == 0:
                return uext_ref[pl.ds(row + SUBLANES * idx, SUBLANES), cs]
            return uph_ref[p, SUBLANES * (idx + 1):SUBLANES * (idx + 2), cs]

        tails = [None] * CONV_CHAINS
        for cbi in range(CONV_WIDTH // LANES):
            cs = slice(cbi * LANES, (cbi + 1) * LANES)
            bias = jnp.broadcast_to(dwb_ref[:, cs], (SUBLANES, LANES))
            wts = [jnp.broadcast_to(dww_ref[j:j + 1, cs], (SUBLANES, LANES)) for j in range(CONV_K)]
            out = []
            for m0 in range(0, n_acc, CONV_CHAINS):
                accs = [bias if tails[c] is None else bias + _ordered_zero(tails[c])
                        for c in range(CONV_CHAINS)]
                loaded = {}
                for j in range(CONV_K):
                    p, a = (j + shift) % n_phase, (j + shift) // n_phase
                    for c in range(CONV_CHAINS):
                        key = (p, a + m0 + c)
                        if key not in loaded:
                            loaded[key] = window_vreg(p, a + m0 + c, cs)
                        accs[c] = accs[c] + loaded[key] * wts[j]
                tails = accs
                out.extend(accs)
            conv_ref[pl.ds(row, CONV_RB), cs] = jnp.concatenate(out, axis=0)
        return carry

    lax.fori_loop(0, n_iter, conv_and_project, 0)
    uext_ref[0:HIST, :] = uext_ref[TM:TM + HIST, :]

    q = proj_ref[:, Q0:Q0 + RET_WIDTH]
    k = proj_ref[:, K0:K0 + RET_WIDTH]
    v = proj_ref[:, V0:V0 + RET_WIDTH]
    gr = proj_ref[:, G0:G0 + RET_WIDTH]
    for h in range(RET_HEADS):
        hs = slice(h * HEAD_DIM, (h + 1) * HEAD_DIM)
        q_rot = _rotary(q[:, hs], cos, sin)
        k_rot = _rotary(k[:, hs], cos, sin)
        qdec = jnp.concatenate([qdec_ref[h], qdec_ref[h]], axis=-1)
        for c in range(TM // CHUNK):
            rs = slice(c * CHUNK, (c + 1) * CHUNK)
            qc, kc, vc = q_rot[rs], k_rot[rs], v[rs, hs]
            scores = _dot_nt(qc.astype(BF16), kc.astype(BF16)) * mask_ref[h]
            inner = _dot(scores.astype(BF16), vc.astype(BF16))
            cross = _dot((qc * qdec).astype(BF16), state_ref[h].astype(BF16))
            state_update(h, kc, vc)
            y = inner + cross
            mu = jnp.mean(y, axis=-1, keepdims=True)
            var = jnp.mean(jnp.square(y - mu), axis=-1, keepdims=True)
            yn = (y - mu) * lax.rsqrt(var + EPS) * gng_ref[:, hs]
            y_ref[rs, hs] = (yn * jax.nn.silu(gr[rs, hs])).astype(y_ref.dtype)

    uf = conv_ref[...]
    mu = jnp.mean(uf, axis=-1, keepdims=True)
    var = jnp.mean(jnp.square(uf - mu), axis=-1, keepdims=True)
    uf = (uf - mu) * lax.rsqrt(var + EPS) * clng_ref[...] + clnb_ref[...]
    uc = _dot(jax.nn.silu(uf).astype(BF16), pww_ref[...]) + pwb_ref[...]
    gc = _dot(hn, win_ref[:, C0:C0 + CONV_WIDTH])
    y_ref[:, RET_WIDTH:] = (uc * jax.nn.silu(gc)).astype(y_ref.dtype)


def _out_kernel(y_ref, x_ref, wout_hbm, fg_ref, o_ref, wout_ref, stage_ref, sem_ref):
    @pl.when(pl.program_id(0) == 0)
    def _weights():
        _load_cast_weight(wout_hbm, wout_ref, stage_ref, sem_ref, STAGE_ROWS_OUT)

    h = x_ref[...] + _dot(y_ref[...], wout_ref[...])
    o_ref[...] = _rms_norm_rows(h, fg_ref[...])


def _resident(shape):
    nd = len(shape)
    return pl.BlockSpec(shape, lambda i: (0,) * nd, pipeline_mode=pl.Buffered(1))


def kernel(x, meta_tokens, ln_g, w_in, ret_gn_g, conv_dw_w, conv_dw_b, conv_ln_g, conv_ln_b,
           conv_pw_w, conv_pw_b, w_out, final_g):
    assert x.shape == (1, SEQ, D_MODEL) and w_in.shape == (1, D_MODEL, IN_WIDTH)
    tabs, chunk_decay = _constant_tables()
    x2 = x[0]
    row = lambda a: a.reshape(1, -1)
    hbm = pl.BlockSpec(memory_space=pl.ANY)

    mix_in = [
        (x2, pl.BlockSpec((TM, D_MODEL), lambda i: (i, 0))),
        (meta_tokens, _resident((N_META, D_MODEL))),
        (row(ln_g[0]), _resident((1, D_MODEL))),
        (w_in[0], hbm),
        (tabs["cos_b"], pl.BlockSpec((1, 8, HALF), lambda i: (i, 0, 0))),
        (tabs["sin_b"], pl.BlockSpec((1, 8, HALF), lambda i: (i, 0, 0))),
        (tabs["cos_b"], pl.BlockSpec((1, 8, HALF), lambda i: (N_TILES, 0, 0))),
        (tabs["sin_b"], pl.BlockSpec((1, 8, HALF), lambda i: (N_TILES, 0, 0))),
        (tabs["cos_o"], _resident((TM, HALF))),
        (tabs["sin_o"], _resident((TM, HALF))),
        (tabs["mask"], _resident((RET_HEADS, CHUNK, CHUNK))),
        (tabs["qdec"], _resident((RET_HEADS, CHUNK, LANES))),
        (tabs["kdec"], _resident((RET_HEADS, CHUNK, LANES))),
        (row(ret_gn_g[0]), _resident((1, RET_WIDTH))),
        (conv_dw_w[0], _resident((CONV_K, CONV_WIDTH))),
        (row(conv_dw_b[0]), _resident((1, CONV_WIDTH))),
        (row(conv_ln_g[0]), _resident((1, CONV_WIDTH))),
        (row(conv_ln_b[0]), _resident((1, CONV_WIDTH))),
        (conv_pw_w[0], hbm),
        (row(conv_pw_b[0]), _resident((1, CONV_WIDTH))),
    ]
    y = pl.pallas_call(
        functools.partial(_mix_kernel, chunk_decay),
        grid=(N_TILES,),
        in_specs=[s for _, s in mix_in],
        out_specs=pl.BlockSpec((TM, D_MODEL), lambda i: (i, 0)),
        out_shape=jax.ShapeDtypeStruct((SEQ, D_MODEL), BF16),
        scratch_shapes=[
            pltpu.VMEM((D_MODEL, IN_WIDTH), BF16),
            pltpu.VMEM((CONV_WIDTH, CONV_WIDTH), BF16),
            pltpu.SemaphoreType.DMA((TM // STAGE_ROWS,)),
            pltpu.VMEM((RET_HEADS, HEAD_DIM, HEAD_DIM), F32),
            pltpu.VMEM((HIST + TM, CONV_WIDTH), F32),
            pltpu.VMEM((SUBLANES, SUBLANES + HIST + CONV_RB, CONV_WIDTH), F32),
            pltpu.VMEM((TM, 4 * RET_WIDTH), F32),
            pltpu.VMEM((TM, CONV_WIDTH), F32),
            pltpu.VMEM((TM, D_MODEL), BF16),
        ],
        compiler_params=pltpu.CompilerParams(
            dimension_semantics=("arbitrary",), vmem_limit_bytes=VMEM_LIMIT_MIX),
        name="mix",
    )(*[a for a, _ in mix_in])

    out = pl.pallas_call(
        _out_kernel,
        grid=(SEQ // TM_OUT,),
        in_specs=[
            pl.BlockSpec((TM_OUT, D_MODEL), lambda i: (i, 0)),
            pl.BlockSpec((TM_OUT, D_MODEL), lambda i: (i, 0)),
            hbm,
            _resident((1, D_MODEL)),
        ],
        out_specs=pl.BlockSpec((TM_OUT, D_MODEL), lambda i: (i, 0)),
        out_shape=jax.ShapeDtypeStruct((SEQ, D_MODEL), x.dtype),
        scratch_shapes=[
            pltpu.VMEM((D_MODEL, D_MODEL), BF16),
            pltpu.VMEM((STAGE_SLOTS_OUT * STAGE_ROWS_OUT, D_MODEL), F32),
            pltpu.SemaphoreType.DMA((STAGE_SLOTS_OUT,)),
        ],
        compiler_params=pltpu.CompilerParams(
            dimension_semantics=("arbitrary",), vmem_limit_bytes=VMEM_LIMIT_OUT),
        name="out",
    )(y, x2, w_out[0], row(final_g))
    return out[None]
```

```python
import functools

import numpy as np
import jax
import jax.numpy as jnp
from jax import lax
from jax.experimental import pallas as pl
from jax.experimental.pallas import tpu as pltpu

D_MODEL = 2048
SEQ = 8192
N_META = 16
RET_WIDTH = 1024
RET_HEADS = 4
HEAD_DIM = 256
HALF = HEAD_DIM // 2
CONV_WIDTH = 1024
CONV_K = 31
CHUNK = 128
ROPE_BASE = 10000.0
EPS = 1e-6
IN_WIDTH = 4 * RET_WIDTH + 3 * CONV_WIDTH

Q0, K0, V0, G0, A0, B0, C0 = 0, 1024, 2048, 3072, 4096, 5120, 6144
PROJ_GC = 4 * RET_WIDTH

LANES = 128
SUBLANES = 8
TM = 256
N_TILES = SEQ // TM
HIST = 32
TM_OUT = 512
CONV_RB = 128
CONV_CHAINS = 2
STAGE_ROWS = 32
STAGE_ROWS_OUT = 64
STAGE_SLOTS_OUT = 8
VMEM_LIMIT_MIX = 58 * 1024 * 1024
VMEM_LIMIT_OUT = 48 * 1024 * 1024

BF16 = jnp.bfloat16
F32 = jnp.float32


def _constant_tables():
    half = HALF
    inv_freq = ROPE_BASE ** (-np.arange(half, dtype=np.float64) / half)
    bases = np.concatenate([N_META + TM * np.arange(N_TILES, dtype=np.float64),
                            [-(CHUNK - N_META)]])
    ang_b = bases[:, None] * inv_freq[None, :]
    ang_o = np.arange(TM, dtype=np.float64)[:, None] * inv_freq[None, :]
    cos_b = np.broadcast_to(np.cos(ang_b)[:, None, :], (N_TILES + 1, 8, half))
    sin_b = np.broadcast_to(np.sin(ang_b)[:, None, :], (N_TILES + 1, 8, half))
    gamma = 1.0 - np.exp2(-5.0 - np.arange(RET_HEADS, dtype=np.float64))
    log_g = np.log(gamma)
    idx = np.arange(CHUNK, dtype=np.float64)
    rel = idx[:, None] - idx[None, :]
    scale = HEAD_DIM ** -0.5
    mask = np.where(rel[None] >= 0, np.exp(np.maximum(rel, 0.0)[None] * log_g[:, None, None]), 0.0)
    q_decay = np.exp((idx[None, :] + 1.0) * log_g[:, None])
    k_decay = np.exp((CHUNK - 1.0 - idx[None, :]) * log_g[:, None])
    chunk_decay = np.exp(CHUNK * log_g)
    qdec = np.broadcast_to(q_decay[:, :, None], (RET_HEADS, CHUNK, LANES))
    kdec = np.broadcast_to((k_decay * scale)[:, :, None], (RET_HEADS, CHUNK, LANES))
    f = lambda a: jnp.asarray(np.ascontiguousarray(a), dtype=F32)
    tabs = dict(cos_b=f(cos_b), sin_b=f(sin_b), cos_o=f(np.cos(ang_o)), sin_o=f(np.sin(ang_o)),
                mask=f(mask * scale), qdec=f(qdec), kdec=f(kdec))
    return tabs, tuple(float(c) for c in chunk_decay)


def _rms_norm_rows(xf, g):
    ms = jnp.mean(xf * xf, axis=-1, keepdims=True)
    return xf * lax.rsqrt(ms + EPS) * g


def _rotary(x, cos, sin):
    x1, x2 = x[:, :HALF], x[:, HALF:]
    return jnp.concatenate([x1 * cos - x2 * sin, x1 * sin + x2 * cos], axis=-1)


def _ordered_zero(v):
    bits = lax.bitcast_convert_type(v, jnp.uint32)
    return lax.bitcast_convert_type((bits >> 16) >> 16, F32)


def _dot(a, b):
    return jnp.dot(a, b, preferred_element_type=F32)


def _dot_nt(a, b):
    return lax.dot_general(a, b, (((1,), (1,)), ((), ())), preferred_element_type=F32)


def _dot_tn(a, b):
    return lax.dot_general(a, b, (((0,), (0,)), ((), ())), preferred_element_type=F32)


def _load_cast_weight(w_hbm, w_vmem, stage_ref, sem_ref, rows):
    n_slots = stage_ref.shape[0] // rows
    n_chunks = w_hbm.shape[0] // rows
    assert sem_ref.shape[0] >= n_slots and n_chunks >= n_slots

    def slot_rows(slot):
        return pl.ds(pl.multiple_of(slot * rows, rows), rows)

    def chunk_copy(c, slot):
        return pltpu.make_async_copy(w_hbm.at[pl.ds(c * rows, rows), :],
                                     stage_ref.at[slot_rows(slot), :], sem_ref.at[slot])

    for c in range(n_slots - 1):
        chunk_copy(c, c).start()

    def body(c, carry):
        slot = c % n_slots
        chunk_copy(c, slot).wait()
        nxt = c + n_slots - 1

        @pl.when(nxt < n_chunks)
        def _prefetch():
            chunk_copy(nxt, nxt % n_slots).start()

        r0 = pl.multiple_of(c * rows, rows)
        w_vmem[pl.ds(r0, rows), :] = stage_ref[slot_rows(slot), :].astype(BF16)
        return carry

    lax.fori_loop(0, n_chunks, body, 0)


def _mix_kernel(chunk_decay,
                x_ref, meta_ref, lng_ref, win_hbm, cosb_ref, sinb_ref, cosm_ref, sinm_ref,
                coso_ref, sino_ref, mask_ref, qdec_ref, kdec_ref, gng_ref,
                dww_ref, dwb_ref, clng_ref, clnb_ref, pww_hbm, pwb_ref,
                y_ref,
                win_ref, pww_ref, sem_ref,
                state_ref, uext_ref, uph_ref, proj_ref, conv_ref, hn_ref):
    i = pl.program_id(0)
    n_phase = SUBLANES

    def state_update(h, k_rot, v):
        kd = (k_rot * jnp.concatenate([kdec_ref[h], kdec_ref[h]], axis=-1)).astype(BF16)
        state_ref[h] = state_ref[h] * chunk_decay[h] + _dot_tn(kd, v.astype(BF16))

    @pl.when(i == 0)
    def _meta():
        stage_w = proj_ref.shape[1]
        for c0 in range(0, IN_WIDTH, stage_w):
            cols = pl.ds(c0, min(stage_w, IN_WIDTH - c0))
            _load_cast_weight(win_hbm.at[:, cols], win_ref.at[:, cols],
                              proj_ref.at[:, pl.ds(0, cols.size)], sem_ref, STAGE_ROWS)
        _load_cast_weight(pww_hbm, pww_ref, proj_ref.at[:, pl.ds(0, CONV_WIDTH)], sem_ref,
                          STAGE_ROWS)
        state_ref[...] = jnp.zeros_like(state_ref)
        hn_ref[0:CHUNK - N_META, :] = jnp.zeros((CHUNK - N_META, D_MODEL), BF16)
        hn_ref[CHUNK - N_META:CHUNK, :] = _rms_norm_rows(meta_ref[...], lng_ref[...]).astype(BF16)
        hm = hn_ref[0:CHUNK, :]
        cm, sm = cosm_ref[0][0:1, :], sinm_ref[0][0:1, :]
        cos = cm * coso_ref[:CHUNK] - sm * sino_ref[:CHUNK]
        sin = sm * coso_ref[:CHUNK] + cm * sino_ref[:CHUNK]
        k = _dot(hm, win_ref[:, K0:K0 + RET_WIDTH])
        v = _dot(hm, win_ref[:, V0:V0 + RET_WIDTH])
        for h in range(RET_HEADS):
            hs = slice(h * HEAD_DIM, (h + 1) * HEAD_DIM)
            state_update(h, _rotary(k[:, hs], cos, sin), v[:, hs])
        a = _dot(hm, win_ref[:, A0:A0 + CONV_WIDTH])
        b = _dot(hm, win_ref[:, B0:B0 + CONV_WIDTH])
        u = a * jax.nn.sigmoid(b)
        uext_ref[0:HIST, :] = u[CHUNK - HIST:, :]

    hn_ref[...] = _rms_norm_rows(x_ref[...], lng_ref[...]).astype(BF16)
    hn = hn_ref[...]

    cb, sb = cosb_ref[0][0:1, :], sinb_ref[0][0:1, :]
    cos = cb * coso_ref[...] - sb * sino_ref[...]
    sin = sb * coso_ref[...] + cb * sino_ref[...]

    a = _dot(hn, win_ref[:, A0:A0 + CONV_WIDTH])
    b = _dot(hn, win_ref[:, B0:B0 + CONV_WIDTH])
    uext_ref[HIST:HIST + TM, :] = a * jax.nn.sigmoid(b)

    shift = HIST - (CONV_K - 1)
    n_acc = CONV_RB // SUBLANES
    n_iter = TM // CONV_RB
    proj_slice = 4 * RET_WIDTH // n_iter
    gate_slice = CONV_WIDTH // n_iter

    def conv_and_project(t, carry):
        pcol = pl.multiple_of(t * proj_slice, proj_slice)
        proj_ref[:, pl.ds(pcol, proj_slice)] = _dot(hn_ref[...],
                                                    win_ref[:, pl.ds(Q0 + pcol, proj_slice)])
        gcol = pl.multiple_of(t * gate_slice, gate_slice)
        proj_ref[:, pl.ds(PROJ_GC + gcol, gate_slice)] = _dot(
            hn_ref[...], win_ref[:, pl.ds(C0 + gcol, gate_slice)])
        row = pl.multiple_of(t * CONV_RB, CONV_RB)
        window = uext_ref[pl.ds(row, HIST + CONV_RB), :]
        for p in range(1, n_phase):
            uph_ref[p, SUBLANES - p:SUBLANES - p + HIST + CONV_RB, :] = window

        def window_vreg(p, idx, cs):
            if p == 0:
                return uext_ref[pl.ds(row + SUBLANES * idx, SUBLANES), cs]
            return uph_ref[p, SUBLANES * (idx + 1):SUBLANES * (idx + 2), cs]

        tails = [None] * CONV_CHAINS
        for cbi in range(CONV_WIDTH // LANES):
            cs = slice(cbi * LANES, (cbi + 1) * LANES)
            bias = jnp.broadcast_to(dwb_ref[:, cs], (SUBLANES, LANES))
            wts = [jnp.broadcast_to(dww_ref[j:j + 1, cs], (SUBLANES, LANES)) for j in range(CONV_K)]
            out = []
            for m0 in range(0, n_acc, CONV_CHAINS):
                width = min(CONV_CHAINS, n_acc - m0)
                accs = [bias if tails[c] is None else bias + _ordered_zero(tails[c])
                        for c in range(width)]
                loaded = {}
                for j in range(CONV_K):
                    p, a = (j + shift) % n_phase, (j + shift) // n_phase
                    for c in range(width):
                        key = (p, a + m0 + c)
                        if key not in loaded:
                            loaded[key] = window_vreg(p, a + m0 + c, cs)
                        accs[c] = accs[c] + loaded[key] * wts[j]
                tails[:width] = accs
                out.extend(accs)
            conv_ref[pl.ds(row, CONV_RB), cs] = jnp.concatenate(out, axis=0)
        return carry

    lax.fori_loop(0, n_iter, conv_and_project, 0)
    uext_ref[0:HIST, :] = uext_ref[TM:TM + HIST, :]

    q = proj_ref[:, Q0:Q0 + RET_WIDTH]
    k = proj_ref[:, K0:K0 + RET_WIDTH]
    v = proj_ref[:, V0:V0 + RET_WIDTH]
    gr = proj_ref[:, G0:G0 + RET_WIDTH]
    for h in range(RET_HEADS):
        hs = slice(h * HEAD_DIM, (h + 1) * HEAD_DIM)
        q_rot = _rotary(q[:, hs], cos, sin)
        k_rot = _rotary(k[:, hs], cos, sin)
        qdec = jnp.concatenate([qdec_ref[h], qdec_ref[h]], axis=-1)
        for c in range(TM // CHUNK):
            rs = slice(c * CHUNK, (c + 1) * CHUNK)
            qc, kc, vc = q_rot[rs], k_rot[rs], v[rs, hs]
            scores = _dot_nt(qc.astype(BF16), kc.astype(BF16)) * mask_ref[h]
            inner = _dot(scores.astype(BF16), vc.astype(BF16))
            cross = _dot((qc * qdec).astype(BF16), state_ref[h].astype(BF16))
            state_update(h, kc, vc)
            y = inner + cross
            mu = jnp.mean(y, axis=-1, keepdims=True)
            var = jnp.mean(jnp.square(y - mu), axis=-1, keepdims=True)
            yn = (y - mu) * lax.rsqrt(var + EPS) * gng_ref[:, hs]
            y_ref[rs, hs] = (yn * jax.nn.silu(gr[rs, hs])).astype(y_ref.dtype)

    uf = conv_ref[...]
    mu = jnp.mean(uf, axis=-1, keepdims=True)
    var = jnp.mean(jnp.square(uf - mu), axis=-1, keepdims=True)
    uf = (uf - mu) * lax.rsqrt(var + EPS) * clng_ref[...] + clnb_ref[...]
    uc = _dot(jax.nn.silu(uf).astype(BF16), pww_ref[...]) + pwb_ref[...]
    gc = proj_ref[:, PROJ_GC:PROJ_GC + CONV_WIDTH]
    y_ref[:, RET_WIDTH:] = (uc * jax.nn.silu(gc)).astype(y_ref.dtype)


def _out_kernel(y_ref, x_ref, wout_hbm, fg_ref, o_ref, wout_ref, stage_ref, sem_ref):
    @pl.when(pl.program_id(0) == 0)
    def _weights():
        _load_cast_weight(wout_hbm, wout_ref, stage_ref, sem_ref, STAGE_ROWS_OUT)

    h = x_ref[...] + _dot(y_ref[...], wout_ref[...])
    o_ref[...] = _rms_norm_rows(h, fg_ref[...])


def _resident(shape):
    nd = len(shape)
    return pl.BlockSpec(shape, lambda i: (0,) * nd, pipeline_mode=pl.Buffered(1))


def kernel(x, meta_tokens, ln_g, w_in, ret_gn_g, conv_dw_w, conv_dw_b, conv_ln_g, conv_ln_b,
           conv_pw_w, conv_pw_b, w_out, final_g):
    assert x.shape == (1, SEQ, D_MODEL) and w_in.shape == (1, D_MODEL, IN_WIDTH)
    tabs, chunk_decay = _constant_tables()
    x2 = x[0]
    row = lambda a: a.reshape(1, -1)
    hbm = pl.BlockSpec(memory_space=pl.ANY)

    mix_in = [
        (x2, pl.BlockSpec((TM, D_MODEL), lambda i: (i, 0))),
        (meta_tokens, _resident((N_META, D_MODEL))),
        (row(ln_g[0]), _resident((1, D_MODEL))),
        (w_in[0], hbm),
        (tabs["cos_b"], pl.BlockSpec((1, 8, HALF), lambda i: (i, 0, 0))),
        (tabs["sin_b"], pl.BlockSpec((1, 8, HALF), lambda i: (i, 0, 0))),
        (tabs["cos_b"], pl.BlockSpec((1, 8, HALF), lambda i: (N_TILES, 0, 0))),
        (tabs["sin_b"], pl.BlockSpec((1, 8, HALF), lambda i: (N_TILES, 0, 0))),
        (tabs["cos_o"], _resident((TM, HALF))),
        (tabs["sin_o"], _resident((TM, HALF))),
        (tabs["mask"], _resident((RET_HEADS, CHUNK, CHUNK))),
        (tabs["qdec"], _resident((RET_HEADS, CHUNK, LANES))),
        (tabs["kdec"], _resident((RET_HEADS, CHUNK, LANES))),
        (row(ret_gn_g[0]), _resident((1, RET_WIDTH))),
        (conv_dw_w[0], _resident((CONV_K, CONV_WIDTH))),
        (row(conv_dw_b[0]), _resident((1, CONV_WIDTH))),
        (row(conv_ln_g[0]), _resident((1, CONV_WIDTH))),
        (row(conv_ln_b[0]), _resident((1, CONV_WIDTH))),
        (conv_pw_w[0], hbm),
        (row(conv_pw_b[0]), _resident((1, CONV_WIDTH))),
    ]
    y = pl.pallas_call(
        functools.partial(_mix_kernel, chunk_decay),
        grid=(N_TILES,),
        in_specs=[s for _, s in mix_in],
        out_specs=pl.BlockSpec((TM, D_MODEL), lambda i: (i, 0)),
        out_shape=jax.ShapeDtypeStruct((SEQ, D_MODEL), BF16),
        scratch_shapes=[
            pltpu.VMEM((D_MODEL, IN_WIDTH), BF16),
            pltpu.VMEM((CONV_WIDTH, CONV_WIDTH), BF16),
            pltpu.SemaphoreType.DMA((TM // STAGE_ROWS,)),
            pltpu.VMEM((RET_HEADS, HEAD_DIM, HEAD_DIM), F32),
            pltpu.VMEM((HIST + TM, CONV_WIDTH), F32),
            pltpu.VMEM((SUBLANES, SUBLANES + HIST + CONV_RB, CONV_WIDTH), F32),
            pltpu.VMEM((TM, PROJ_GC + CONV_WIDTH), F32),
            pltpu.VMEM((TM, CONV_WIDTH), F32),
            pltpu.VMEM((TM, D_MODEL), BF16),
        ],
        compiler_params=pltpu.CompilerParams(
            dimension_semantics=("arbitrary",), vmem_limit_bytes=VMEM_LIMIT_MIX),
        name="mix",
    )(*[a for a, _ in mix_in])

    out = pl.pallas_call(
        _out_kernel,
        grid=(SEQ // TM_OUT,),
        in_specs=[
            pl.BlockSpec((TM_OUT, D_MODEL), lambda i: (i, 0)),
            pl.BlockSpec((TM_OUT, D_MODEL), lambda i: (i, 0)),
            hbm,
            _resident((1, D_MODEL)),
        ],
        out_specs=pl.BlockSpec((TM_OUT, D_MODEL), lambda i: (i, 0)),
        out_shape=jax.ShapeDtypeStruct((SEQ, D_MODEL), x.dtype),
        scratch_shapes=[
            pltpu.VMEM((D_MODEL, D_MODEL), BF16),
            pltpu.VMEM((STAGE_SLOTS_OUT * STAGE_ROWS_OUT, D_MODEL), F32),
            pltpu.SemaphoreType.DMA((STAGE_SLOTS_OUT,)),
        ],
        compiler_params=pltpu.CompilerParams(
            dimension_semantics=("arbitrary",), vmem_limit_bytes=VMEM_LIMIT_OUT),
        name="out",
    )(y, x2, w_out[0], row(final_g))
    return out[None]
```

```python
import functools

import numpy as np
import jax
import jax.numpy as jnp
from jax import lax
from jax.experimental import pallas as pl
from jax.experimental.pallas import tpu as pltpu

D_MODEL = 2048
SEQ = 8192
N_META = 16
RET_WIDTH = 1024
RET_HEADS = 4
HEAD_DIM = 256
HALF = HEAD_DIM // 2
CONV_WIDTH = 1024
CONV_K = 31
CHUNK = 128
ROPE_BASE = 10000.0
EPS = 1e-6
IN_WIDTH = 4 * RET_WIDTH + 3 * CONV_WIDTH

Q0, K0, V0, G0, A0, B0, C0 = 0, 1024, 2048, 3072, 4096, 5120, 6144
PROJ_GC = 4 * RET_WIDTH

LANES = 128
SUBLANES = 8
TM = 256
N_TILES = SEQ // TM
HIST = 32
TM_OUT = 512
CONV_RB = 128
SLAB = 16
LN_CHAINS = 4
GATE_CHAINS = 2
ROT_CHAINS = 1
NORM_CHAINS = 8
GLU_BLOCK = 256
TAIL_SLABS_PER_GLU_DOT = 4
RETENTION_SLABS_PER_GLU_DOT = 32
CONV_CHAINS = 2
STAGE_ROWS = 32
STAGE_ROWS_OUT = 64
STAGE_SLOTS_OUT = 8
VMEM_LIMIT_MIX = 58 * 1024 * 1024
VMEM_LIMIT_OUT = 48 * 1024 * 1024

BF16 = jnp.bfloat16
F32 = jnp.float32


def _constant_tables():
    half = HALF
    inv_freq = ROPE_BASE ** (-np.arange(half, dtype=np.float64) / half)
    bases = np.concatenate([N_META + TM * np.arange(N_TILES, dtype=np.float64),
                            [-(CHUNK - N_META)]])
    ang_b = bases[:, None] * inv_freq[None, :]
    ang_o = np.arange(TM, dtype=np.float64)[:, None] * inv_freq[None, :]
    cos_b = np.broadcast_to(np.cos(ang_b)[:, None, :], (N_TILES + 1, 8, half))
    sin_b = np.broadcast_to(np.sin(ang_b)[:, None, :], (N_TILES + 1, 8, half))
    gamma = 1.0 - np.exp2(-5.0 - np.arange(RET_HEADS, dtype=np.float64))
    log_g = np.log(gamma)
    idx = np.arange(CHUNK, dtype=np.float64)
    rel = idx[:, None] - idx[None, :]
    scale = HEAD_DIM ** -0.5
    mask = np.where(rel[None] >= 0, np.exp(np.maximum(rel, 0.0)[None] * log_g[:, None, None]), 0.0)
    q_decay = np.exp((idx[None, :] + 1.0) * log_g[:, None])
    k_decay = np.exp((CHUNK - 1.0 - idx[None, :]) * log_g[:, None])
    chunk_decay = np.exp(CHUNK * log_g)
    qdec = np.broadcast_to(q_decay[:, :, None], (RET_HEADS, CHUNK, LANES))
    kdec = np.broadcast_to((k_decay * scale)[:, :, None], (RET_HEADS, CHUNK, LANES))
    f = lambda a: jnp.asarray(np.ascontiguousarray(a), dtype=F32)
    tabs = dict(cos_b=f(cos_b), sin_b=f(sin_b), cos_o=f(np.cos(ang_o)), sin_o=f(np.sin(ang_o)),
                mask=f(mask * scale), qdec=f(qdec), kdec=f(kdec))
    return tabs, tuple(float(c) for c in chunk_decay)


def _rms_norm_rows(xf, g):
    ms = jnp.mean(xf * xf, axis=-1, keepdims=True)
    return xf * lax.rsqrt(ms + EPS) * g


def _rotary(x, cos, sin):
    x1, x2 = x[:, :HALF], x[:, HALF:]
    return jnp.concatenate([x1 * cos - x2 * sin, x1 * sin + x2 * cos], axis=-1)


def _ordered_zero(v):
    bits = lax.bitcast_convert_type(v, jnp.uint32)
    return lax.bitcast_convert_type((bits >> 16) >> 16, F32)


def _gate_rows(x, zero):
    if zero is None:
        return x
    head = x[:, :LANES] + jnp.concatenate([zero] * (x.shape[0] // SUBLANES), axis=0)
    return jnp.concatenate([head, x[:, LANES:]], axis=1)


def _chained_slabs(n_slabs, n_chains, slab_fn):
    tails = [None] * n_chains
    for s in range(n_slabs):
        c = s % n_chains
        tails[c] = slab_fn(s, None if tails[c] is None else _ordered_zero(tails[c]))
        yield


def _interleave(streams):
    live = list(streams)
    while live:
        for entry in list(live):
            stream, weight = entry
            for _ in range(weight):
                if next(stream, _interleave) is _interleave:
                    live.remove(entry)
                    break


def _dot(a, b):
    return jnp.dot(a, b, preferred_element_type=F32)


def _dot_nt(a, b):
    return lax.dot_general(a, b, (((1,), (1,)), ((), ())), preferred_element_type=F32)


def _dot_tn(a, b):
    return lax.dot_general(a, b, (((0,), (0,)), ((), ())), preferred_element_type=F32)


def _load_cast_weight(w_hbm, w_vmem, stage_ref, sem_ref, rows):
    n_slots = stage_ref.shape[0] // rows
    n_chunks = w_hbm.shape[0] // rows
    assert sem_ref.shape[0] >= n_slots and n_chunks >= n_slots

    def slot_rows(slot):
        return pl.ds(pl.multiple_of(slot * rows, rows), rows)

    def chunk_copy(c, slot):
        return pltpu.make_async_copy(w_hbm.at[pl.ds(c * rows, rows), :],
                                     stage_ref.at[slot_rows(slot), :], sem_ref.at[slot])

    for c in range(n_slots - 1):
        chunk_copy(c, c).start()

    def body(c, carry):
        slot = c % n_slots
        chunk_copy(c, slot).wait()
        nxt = c + n_slots - 1

        @pl.when(nxt < n_chunks)
        def _prefetch():
            chunk_copy(nxt, nxt % n_slots).start()

        r0 = pl.multiple_of(c * rows, rows)
        w_vmem[pl.ds(r0, rows), :] = stage_ref[slot_rows(slot), :].astype(BF16)
        return carry

    lax.fori_loop(0, n_chunks, body, 0)


def _mix_kernel(chunk_decay,
                x_ref, meta_ref, lng_ref, win_hbm, cosb_ref, sinb_ref, cosm_ref, sinm_ref,
                coso_ref, sino_ref, mask_ref, qdec_ref, kdec_ref, gng_ref,
                dww_ref, dwb_ref, clng_ref, clnb_ref, pww_hbm, pwb_ref,
                y_ref,
                win_ref, pww_ref, sem_ref,
                state_ref, uext_ref, uph_ref, proj_ref, conv_ref, hn_ref, act_ref):
    i = pl.program_id(0)

    def state_update(h, k_rot, v):
        kd = (k_rot * jnp.concatenate([kdec_ref[h], kdec_ref[h]], axis=-1)).astype(BF16)
        state_ref[h] = state_ref[h] * chunk_decay[h] + _dot_tn(kd, v.astype(BF16))

    @pl.when(i == 0)
    def _meta():
        stage_w = proj_ref.shape[1]
        for c0 in range(0, IN_WIDTH, stage_w):
            cols = pl.ds(c0, min(stage_w, IN_WIDTH - c0))
            _load_cast_weight(win_hbm.at[:, cols], win_ref.at[:, cols],
                              proj_ref.at[:, pl.ds(0, cols.size)], sem_ref, STAGE_ROWS)
        _load_cast_weight(pww_hbm, pww_ref, proj_ref.at[:, pl.ds(0, CONV_WIDTH)], sem_ref,
                          STAGE_ROWS)
        state_ref[...] = jnp.zeros_like(state_ref)
        conv_ref[...] = jnp.zeros_like(conv_ref)
        proj_ref[...] = jnp.zeros_like(proj_ref)
        hn_ref[0:CHUNK - N_META, :] = jnp.zeros((CHUNK - N_META, D_MODEL), BF16)
        hn_ref[CHUNK - N_META:CHUNK, :] = _rms_norm_rows(meta_ref[...], lng_ref[...]).astype(BF16)
        hm = hn_ref[0:CHUNK, :]
        cm, sm = cosm_ref[0][0:1, :], sinm_ref[0][0:1, :]
        cos = cm * coso_ref[:CHUNK] - sm * sino_ref[:CHUNK]
        sin = sm * coso_ref[:CHUNK] + cm * sino_ref[:CHUNK]
        k = _dot(hm, win_ref[:, K0:K0 + RET_WIDTH])
        v = _dot(hm, win_ref[:, V0:V0 + RET_WIDTH])
        for h in range(RET_HEADS):
            hs = slice(h * HEAD_DIM, (h + 1) * HEAD_DIM)
            state_update(h, _rotary(k[:, hs], cos, sin), v[:, hs])
        a = _dot(hm, win_ref[:, A0:A0 + CONV_WIDTH])
        b = _dot(hm, win_ref[:, B0:B0 + CONV_WIDTH])
        u = a * jax.nn.sigmoid(b)
        uext_ref[0:HIST, :] = u[CHUNK - HIST:, :]

    hn_ref[...] = _rms_norm_rows(x_ref[...], lng_ref[...]).astype(BF16)
    hn = hn_ref[...]

    def glu_projection():
        for c0 in range(0, CONV_WIDTH, GLU_BLOCK):
            a = _dot(hn, win_ref[:, A0 + c0:A0 + c0 + GLU_BLOCK])
            yield
            b = _dot(hn, win_ref[:, B0 + c0:B0 + c0 + GLU_BLOCK])
            uext_ref[HIST:HIST + TM, c0:c0 + GLU_BLOCK] = a * jax.nn.sigmoid(b)
            yield

    def layer_norm_slab(s, zero):
        rows = slice(s * SLAB, (s + 1) * SLAB)
        uf = _gate_rows(conv_ref[rows, :], zero)
        mu = jnp.mean(uf, axis=-1, keepdims=True)
        var = jnp.mean(jnp.square(uf - mu), axis=-1, keepdims=True)
        uf = (uf - mu) * lax.rsqrt(var + EPS) * clng_ref[...] + clnb_ref[...]
        sw = jax.nn.silu(uf)
        act_ref[rows, :] = sw.astype(BF16)
        return sw[:SUBLANES, :LANES]


    def gate_slab(s, zero):
        rows = slice(s * SLAB, (s + 1) * SLAB)
        gc = proj_ref[rows, PROJ_GC:PROJ_GC + CONV_WIDTH]
        if zero is not None:
            gc = gc + jnp.tile(zero, (SLAB // SUBLANES, CONV_WIDTH // LANES))
        out = (conv_ref[rows, :] + pwb_ref[...]) * jax.nn.silu(gc)
        y_ref[rows, RET_WIDTH:] = out.astype(y_ref.dtype)
        return out[:SUBLANES, :LANES]

    def conv_tail():
        yield from _chained_slabs(TM // SLAB, LN_CHAINS, layer_norm_slab)
        conv_ref[...] = _dot(act_ref[...], pww_ref[...])
        yield from _chained_slabs(TM // SLAB, GATE_CHAINS, gate_slab)

    retention = _retention_tail(i, chunk_decay, proj_ref, cosb_ref, sinb_ref, coso_ref, sino_ref,
                                mask_ref, qdec_ref, kdec_ref, gng_ref, state_ref, y_ref)
    _interleave([(glu_projection(), 1), (conv_tail(), TAIL_SLABS_PER_GLU_DOT),
                 (retention, RETENTION_SLABS_PER_GLU_DOT)])

    @pl.when(i < N_TILES)
    def _tile():
        _tile_body(hn_ref, win_ref, dww_ref, dwb_ref, uext_ref, uph_ref, proj_ref, conv_ref)


def _tile_body(hn_ref, win_ref, dww_ref, dwb_ref, uext_ref, uph_ref, proj_ref, conv_ref):
    n_phase = SUBLANES

    shift = HIST - (CONV_K - 1)
    n_acc = CONV_RB // SUBLANES
    n_iter = TM // CONV_RB
    proj_slice = 4 * RET_WIDTH // n_iter
    gate_slice = CONV_WIDTH // n_iter

    def conv_and_project(t, carry):
        pcol = pl.multiple_of(t * proj_slice, proj_slice)
        proj_ref[:, pl.ds(pcol, proj_slice)] = _dot(hn_ref[...],
                                                    win_ref[:, pl.ds(Q0 + pcol, proj_slice)])
        gcol = pl.multiple_of(t * gate_slice, gate_slice)
        proj_ref[:, pl.ds(PROJ_GC + gcol, gate_slice)] = _dot(
            hn_ref[...], win_ref[:, pl.ds(C0 + gcol, gate_slice)])
        row = pl.multiple_of(t * CONV_RB, CONV_RB)
        window = uext_ref[pl.ds(row, HIST + CONV_RB), :]
        for p in range(1, n_phase):
            uph_ref[p, SUBLANES - p:SUBLANES - p + HIST + CONV_RB, :] = window

        def window_vreg(p, idx, cs):
            if p == 0:
                return uext_ref[pl.ds(row + SUBLANES * idx, SUBLANES), cs]
            return uph_ref[p, SUBLANES * (idx + 1):SUBLANES * (idx + 2), cs]

        tails = [None] * CONV_CHAINS
        for cbi in range(CONV_WIDTH // LANES):
            cs = slice(cbi * LANES, (cbi + 1) * LANES)
            bias = jnp.broadcast_to(dwb_ref[:, cs], (SUBLANES, LANES))
            wts = [jnp.broadcast_to(dww_ref[j:j + 1, cs], (SUBLANES, LANES)) for j in range(CONV_K)]
            out = []
            for m0 in range(0, n_acc, CONV_CHAINS):
                width = min(CONV_CHAINS, n_acc - m0)
                accs = [bias if tails[c] is None else bias + _ordered_zero(tails[c])
                        for c in range(width)]
                loaded = {}
                for j in range(CONV_K):
                    p, a = (j + shift) % n_phase, (j + shift) // n_phase
                    for c in range(width):
                        key = (p, a + m0 + c)
                        if key not in loaded:
                            loaded[key] = window_vreg(p, a + m0 + c, cs)
                        accs[c] = accs[c] + loaded[key] * wts[j]
                tails[:width] = accs
                out.extend(accs)
            conv_ref[pl.ds(row, CONV_RB), cs] = jnp.concatenate(out, axis=0)
        return carry

    lax.fori_loop(0, n_iter, conv_and_project, 0)
    uext_ref[0:HIST, :] = uext_ref[TM:TM + HIST, :]


def _retention_tail(i, chunk_decay, proj_ref, cosp_ref, sinp_ref, coso_ref, sino_ref, mask_ref,
                    qdec_ref, kdec_ref, gng_ref, state_ref, y_ref):
    n_slabs = TM // SLAB
    per_chunk = CHUNK // SLAB
    cb, sb = cosp_ref[0][0:1, :], sinp_ref[0][0:1, :]
    cos = cb * coso_ref[...] - sb * sino_ref[...]
    sin = sb * coso_ref[...] + cb * sino_ref[...]
    twice = lambda t: jnp.concatenate([t, t], axis=-1)

    q_bf, k_bf, qd_bf, kd_bf, v_bf = ([[None] * n_slabs for _ in range(RET_HEADS)] for _ in range(5))

    def rotary_slab(u, zero):
        h, s = divmod(u, n_slabs)
        rows = slice(s * SLAB, (s + 1) * SLAB)
        crow = slice((s % per_chunk) * SLAB, (s % per_chunk + 1) * SLAB)
        q_rot = _rotary(_gate_rows(proj_ref[rows, Q0 + h * HEAD_DIM:Q0 + (h + 1) * HEAD_DIM], zero),
                        cos[rows], sin[rows])
        k_rot = _rotary(_gate_rows(proj_ref[rows, K0 + h * HEAD_DIM:K0 + (h + 1) * HEAD_DIM], zero),
                        cos[rows], sin[rows])
        qd = q_rot * twice(qdec_ref[h, crow, :])
        q_bf[h][s], k_bf[h][s] = q_rot.astype(BF16), k_rot.astype(BF16)
        qd_bf[h][s] = qd.astype(BF16)
        kd_bf[h][s] = (k_rot * twice(kdec_ref[h, crow, :])).astype(BF16)
        v_bf[h][s] = proj_ref[rows, V0 + h * HEAD_DIM:V0 + (h + 1) * HEAD_DIM].astype(BF16)
        return qd[:SUBLANES, :LANES]

    yield from _chained_slabs(RET_HEADS * n_slabs, ROT_CHAINS, rotary_slab)

    for h in range(RET_HEADS):
        hs = slice(h * HEAD_DIM, (h + 1) * HEAD_DIM)
        decay = jnp.where(i == 0, 1.0, chunk_decay[h])
        for c in range(TM // CHUNK):
            chunk = lambda pieces: jnp.concatenate(pieces[h][c * per_chunk:(c + 1) * per_chunk], axis=0)
            qc, kc, qdc, kdc, vc = (chunk(p) for p in (q_bf, k_bf, qd_bf, kd_bf, v_bf))
            scores = _dot_nt(qc, kc) * mask_ref[h]
            inner = _dot(scores.astype(BF16), vc)
            state = state_ref[h]
            cross = _dot(qdc, state.astype(BF16))
            update = _dot_tn(kdc, vc)

            def state_slab(s, zero):
                rows = slice(s * SLAB, (s + 1) * SLAB)
                scale = decay if zero is None else jnp.tile(decay + zero, (SLAB // SUBLANES, 2))
                new = state[rows] * scale + update[rows]
                state_ref[h, rows, :] = new
                return new[:SUBLANES, :LANES]

            yield from _chained_slabs(HEAD_DIM // SLAB, 1, state_slab)
            y = inner + cross

            def norm_slab(s, zero):
                rows = slice(s * SLAB, (s + 1) * SLAB)
                trow = slice(c * CHUNK + s * SLAB, c * CHUNK + (s + 1) * SLAB)
                ys = _gate_rows(y[rows], zero)
                mu = jnp.mean(ys, axis=-1, keepdims=True)
                var = jnp.mean(jnp.square(ys - mu), axis=-1, keepdims=True)
                yn = (ys - mu) * lax.rsqrt(var + EPS) * gng_ref[:, hs]
                gate = proj_ref[trow, G0 + h * HEAD_DIM:G0 + (h + 1) * HEAD_DIM]
                out = yn * jax.nn.silu(gate)
                y_ref[trow, hs] = out.astype(y_ref.dtype)
                return out[:SUBLANES, :LANES]

            yield from _chained_slabs(per_chunk, NORM_CHAINS, norm_slab)


def _out_kernel(y_ref, x_ref, wout_hbm, fg_ref, o_ref, wout_ref, stage_ref, sem_ref):
    @pl.when(pl.program_id(0) == 0)
    def _weights():
        _load_cast_weight(wout_hbm, wout_ref, stage_ref, sem_ref, STAGE_ROWS_OUT)

    h = x_ref[...] + _dot(y_ref[...], wout_ref[...])
    o_ref[...] = _rms_norm_rows(h, fg_ref[...])


def _resident(shape):
    nd = len(shape)
    return pl.BlockSpec(shape, lambda i: (0,) * nd, pipeline_mode=pl.Buffered(1))


def kernel(x, meta_tokens, ln_g, w_in, ret_gn_g, conv_dw_w, conv_dw_b, conv_ln_g, conv_ln_b,
           conv_pw_w, conv_pw_b, w_out, final_g):
    assert x.shape == (1, SEQ, D_MODEL) and w_in.shape == (1, D_MODEL, IN_WIDTH)
    tabs, chunk_decay = _constant_tables()
    x2 = x[0]
    row = lambda a: a.reshape(1, -1)
    hbm = pl.BlockSpec(memory_space=pl.ANY)

    tile = lambda i: jnp.minimum(i, N_TILES - 1)
    prev_tile = lambda i: jnp.maximum(i - 1, 0)
    mix_in = [
        (x2, pl.BlockSpec((TM, D_MODEL), lambda i: (tile(i), 0))),
        (meta_tokens, _resident((N_META, D_MODEL))),
        (row(ln_g[0]), _resident((1, D_MODEL))),
        (w_in[0], hbm),
        (tabs["cos_b"], pl.BlockSpec((1, 8, HALF), lambda i: (prev_tile(i), 0, 0))),
        (tabs["sin_b"], pl.BlockSpec((1, 8, HALF), lambda i: (prev_tile(i), 0, 0))),
        (tabs["cos_b"], pl.BlockSpec((1, 8, HALF), lambda i: (N_TILES, 0, 0))),
        (tabs["sin_b"], pl.BlockSpec((1, 8, HALF), lambda i: (N_TILES, 0, 0))),
        (tabs["cos_o"], _resident((TM, HALF))),
        (tabs["sin_o"], _resident((TM, HALF))),
        (tabs["mask"], _resident((RET_HEADS, CHUNK, CHUNK))),
        (tabs["qdec"], _resident((RET_HEADS, CHUNK, LANES))),
        (tabs["kdec"], _resident((RET_HEADS, CHUNK, LANES))),
        (row(ret_gn_g[0]), _resident((1, RET_WIDTH))),
        (conv_dw_w[0], _resident((CONV_K, CONV_WIDTH))),
        (row(conv_dw_b[0]), _resident((1, CONV_WIDTH))),
        (row(conv_ln_g[0]), _resident((1, CONV_WIDTH))),
        (row(conv_ln_b[0]), _resident((1, CONV_WIDTH))),
        (conv_pw_w[0], hbm),
        (row(conv_pw_b[0]), _resident((1, CONV_WIDTH))),
    ]
    y = pl.pallas_call(
        functools.partial(_mix_kernel, chunk_decay),
        grid=(N_TILES + 1,),
        in_specs=[s for _, s in mix_in],
        out_specs=pl.BlockSpec((TM, D_MODEL), lambda i: (prev_tile(i), 0)),
        out_shape=jax.ShapeDtypeStruct((SEQ, D_MODEL), BF16),
        scratch_shapes=[
            pltpu.VMEM((D_MODEL, IN_WIDTH), BF16),
            pltpu.VMEM((CONV_WIDTH, CONV_WIDTH), BF16),
            pltpu.SemaphoreType.DMA((TM // STAGE_ROWS,)),
            pltpu.VMEM((RET_HEADS, HEAD_DIM, HEAD_DIM), F32),
            pltpu.VMEM((HIST + TM, CONV_WIDTH), F32),
            pltpu.VMEM((SUBLANES, SUBLANES + HIST + CONV_RB, CONV_WIDTH), F32),
            pltpu.VMEM((TM, PROJ_GC + CONV_WIDTH), F32),
            pltpu.VMEM((TM, CONV_WIDTH), F32),
            pltpu.VMEM((TM, D_MODEL), BF16),
            pltpu.VMEM((TM, CONV_WIDTH), BF16),
        ],
        compiler_params=pltpu.CompilerParams(
            dimension_semantics=("arbitrary",), vmem_limit_bytes=VMEM_LIMIT_MIX),
        name="mix",
    )(*[a for a, _ in mix_in])

    out = pl.pallas_call(
        _out_kernel,
        grid=(SEQ // TM_OUT,),
        in_specs=[
            pl.BlockSpec((TM_OUT, D_MODEL), lambda i: (i, 0)),
            pl.BlockSpec((TM_OUT, D_MODEL), lambda i: (i, 0)),
            hbm,
            _resident((1, D_MODEL)),
        ],
        out_specs=pl.BlockSpec((TM_OUT, D_MODEL), lambda i: (i, 0)),
        out_shape=jax.ShapeDtypeStruct((SEQ, D_MODEL), x.dtype),
        scratch_shapes=[
            pltpu.VMEM((D_MODEL, D_MODEL), BF16),
            pltpu.VMEM((STAGE_SLOTS_OUT * STAGE_ROWS_OUT, D_MODEL), F32),
            pltpu.SemaphoreType.DMA((STAGE_SLOTS_OUT,)),
        ],
        compiler_params=pltpu.CompilerParams(
            dimension_semantics=("arbitrary",), vmem_limit_bytes=VMEM_LIMIT_OUT),
        name="out",
    )(y, x2, w_out[0], row(final_g))
    return out[None]
```

```python
import functools

import numpy as np
import jax
import jax.numpy as jnp
from jax import lax
from jax.experimental import pallas as pl
from jax.experimental.pallas import tpu as pltpu

D_MODEL = 2048
SEQ = 8192
N_META = 16
RET_WIDTH = 1024
RET_HEADS = 4
HEAD_DIM = 256
HALF = HEAD_DIM // 2
CONV_WIDTH = 1024
CONV_K = 31
CHUNK = 128
ROPE_BASE = 10000.0
EPS = 1e-6
IN_WIDTH = 4 * RET_WIDTH + 3 * CONV_WIDTH

Q0, K0, V0, G0, A0, B0, C0 = 0, 1024, 2048, 3072, 4096, 5120, 6144
PROJ_GC = 4 * RET_WIDTH

LANES = 128
SUBLANES = 8
TM = 256
N_TILES = SEQ // TM
HIST = 32
TM_OUT = 512
CONV_RB = 128
SLAB = 16
LN_CHAINS = 4
GATE_CHAINS = 2
ROT_CHAINS = 2
NORM_CHAINS = 8
GLU_BLOCK = 256
TAIL_SLABS_PER_GLU_DOT = 3
RETENTION_SLABS_PER_GLU_DOT = 24
CONV_CHAINS = 2
STAGE_ROWS = 32
WOUT_ROWS = D_MODEL // N_TILES
VMEM_LIMIT_MIX = 60 * 1024 * 1024
VMEM_LIMIT_OUT = 48 * 1024 * 1024

BF16 = jnp.bfloat16
F32 = jnp.float32


def _constant_tables():
    half = HALF
    inv_freq = ROPE_BASE ** (-np.arange(half, dtype=np.float64) / half)
    bases = np.concatenate([N_META + TM * np.arange(N_TILES, dtype=np.float64),
                            [-(CHUNK - N_META)]])
    ang_b = bases[:, None] * inv_freq[None, :]
    ang_o = np.arange(TM, dtype=np.float64)[:, None] * inv_freq[None, :]
    cos_b = np.broadcast_to(np.cos(ang_b)[:, None, :], (N_TILES + 1, 8, half))
    sin_b = np.broadcast_to(np.sin(ang_b)[:, None, :], (N_TILES + 1, 8, half))
    gamma = 1.0 - np.exp2(-5.0 - np.arange(RET_HEADS, dtype=np.float64))
    log_g = np.log(gamma)
    idx = np.arange(CHUNK, dtype=np.float64)
    rel = idx[:, None] - idx[None, :]
    scale = HEAD_DIM ** -0.5
    mask = np.where(rel[None] >= 0, np.exp(np.maximum(rel, 0.0)[None] * log_g[:, None, None]), 0.0)
    q_decay = np.exp((idx[None, :] + 1.0) * log_g[:, None])
    k_decay = np.exp((CHUNK - 1.0 - idx[None, :]) * log_g[:, None])
    chunk_decay = np.exp(CHUNK * log_g)
    qdec = np.broadcast_to(q_decay[:, :, None], (RET_HEADS, CHUNK, LANES))
    kdec = np.broadcast_to((k_decay * scale)[:, :, None], (RET_HEADS, CHUNK, LANES))
    f = lambda a: jnp.asarray(np.ascontiguousarray(a), dtype=F32)
    tabs = dict(cos_b=f(cos_b), sin_b=f(sin_b), cos_o=f(np.cos(ang_o)), sin_o=f(np.sin(ang_o)),
                mask=f(mask * scale), qdec=f(qdec), kdec=f(kdec))
    return tabs, tuple(float(c) for c in chunk_decay)


def _rms_norm_rows(xf, g):
    ms = jnp.mean(xf * xf, axis=-1, keepdims=True)
    return xf * lax.rsqrt(ms + EPS) * g


def _rotary(x, cos, sin):
    x1, x2 = x[:, :HALF], x[:, HALF:]
    return jnp.concatenate([x1 * cos - x2 * sin, x1 * sin + x2 * cos], axis=-1)


def _ordered_zero(v):
    bits = lax.bitcast_convert_type(v, jnp.uint32)
    return lax.bitcast_convert_type((bits >> 16) >> 16, F32)


def _gate_rows(x, zero):
    if zero is None:
        return x
    head = x[:, :LANES] + jnp.concatenate([zero] * (x.shape[0] // SUBLANES), axis=0)
    return jnp.concatenate([head, x[:, LANES:]], axis=1)


def _chained_slabs(n_slabs, n_chains, slab_fn):
    tails = [None] * n_chains
    for s in range(n_slabs):
        c = s % n_chains
        tails[c] = slab_fn(s, None if tails[c] is None else _ordered_zero(tails[c]))
        yield


def _interleave(streams):
    live = list(streams)
    while live:
        for entry in list(live):
            stream, weight = entry
            for _ in range(weight):
                if next(stream, _interleave) is _interleave:
                    live.remove(entry)
                    break


def _dot(a, b):
    return jnp.dot(a, b, preferred_element_type=F32)


def _dot_nt(a, b):
    return lax.dot_general(a, b, (((1,), (1,)), ((), ())), preferred_element_type=F32)


def _dot_tn(a, b):
    return lax.dot_general(a, b, (((0,), (0,)), ((), ())), preferred_element_type=F32)


def _load_cast_weight(w_hbm, w_vmem, stage_ref, sem_ref, rows):
    n_slots = stage_ref.shape[0] // rows
    n_chunks = w_hbm.shape[0] // rows
    assert sem_ref.shape[0] >= n_slots and n_chunks >= n_slots

    def slot_rows(slot):
        return pl.ds(pl.multiple_of(slot * rows, rows), rows)

    def chunk_copy(c, slot):
        return pltpu.make_async_copy(w_hbm.at[pl.ds(c * rows, rows), :],
                                     stage_ref.at[slot_rows(slot), :], sem_ref.at[slot])

    for c in range(n_slots - 1):
        chunk_copy(c, c).start()

    def body(c, carry):
        slot = c % n_slots
        chunk_copy(c, slot).wait()
        nxt = c + n_slots - 1

        @pl.when(nxt < n_chunks)
        def _prefetch():
            chunk_copy(nxt, nxt % n_slots).start()

        r0 = pl.multiple_of(c * rows, rows)
        w_vmem[pl.ds(r0, rows), :] = stage_ref[slot_rows(slot), :].astype(BF16)
        return carry

    lax.fori_loop(0, n_chunks, body, 0)


def _mix_kernel(chunk_decay,
                x_ref, meta_ref, lng_ref, win_hbm, cosb_ref, sinb_ref, cosm_ref, sinm_ref,
                coso_ref, sino_ref, mask_ref, qdec_ref, kdec_ref, gng_ref,
                dww_ref, dwb_ref, clng_ref, clnb_ref, pww_hbm, pwb_ref, wout_rows_ref,
                y_ref, wout_bf_ref,
                win_ref, pww_ref, sem_ref,
                state_ref, uext_ref, uph_ref, proj_ref, conv_ref, hn_ref, act_ref):
    i = pl.program_id(0)

    def state_update(h, k_rot, v):
        kd = (k_rot * jnp.concatenate([kdec_ref[h], kdec_ref[h]], axis=-1)).astype(BF16)
        state_ref[h] = state_ref[h] * chunk_decay[h] + _dot_tn(kd, v.astype(BF16))

    @pl.when(i == 0)
    def _meta():
        stage_w = proj_ref.shape[1]
        for c0 in range(0, IN_WIDTH, stage_w):
            cols = pl.ds(c0, min(stage_w, IN_WIDTH - c0))
            _load_cast_weight(win_hbm.at[:, cols], win_ref.at[:, cols],
                              proj_ref.at[:, pl.ds(0, cols.size)], sem_ref, STAGE_ROWS)
        _load_cast_weight(pww_hbm, pww_ref, proj_ref.at[:, pl.ds(0, CONV_WIDTH)], sem_ref,
                          STAGE_ROWS)
        state_ref[...] = jnp.zeros_like(state_ref)
        conv_ref[...] = jnp.zeros_like(conv_ref)
        proj_ref[...] = jnp.zeros_like(proj_ref)
        hn_ref[0:CHUNK - N_META, :] = jnp.zeros((CHUNK - N_META, D_MODEL), BF16)
        hn_ref[CHUNK - N_META:CHUNK, :] = _rms_norm_rows(meta_ref[...], lng_ref[...]).astype(BF16)
        hm = hn_ref[0:CHUNK, :]
        cm, sm = cosm_ref[0][0:1, :], sinm_ref[0][0:1, :]
        cos = cm * coso_ref[:CHUNK] - sm * sino_ref[:CHUNK]
        sin = sm * coso_ref[:CHUNK] + cm * sino_ref[:CHUNK]
        k = _dot(hm, win_ref[:, K0:K0 + RET_WIDTH])
        v = _dot(hm, win_ref[:, V0:V0 + RET_WIDTH])
        for h in range(RET_HEADS):
            hs = slice(h * HEAD_DIM, (h + 1) * HEAD_DIM)
            state_update(h, _rotary(k[:, hs], cos, sin), v[:, hs])
        a = _dot(hm, win_ref[:, A0:A0 + CONV_WIDTH])
        b = _dot(hm, win_ref[:, B0:B0 + CONV_WIDTH])
        u = a * jax.nn.sigmoid(b)
        uext_ref[0:HIST, :] = u[CHUNK - HIST:, :]

    wout_bf_ref[...] = wout_rows_ref[...].astype(BF16)

    hn_ref[...] = _rms_norm_rows(x_ref[...], lng_ref[...]).astype(BF16)
    hn = hn_ref[...]

    def glu_projection():
        for c0 in range(0, CONV_WIDTH, GLU_BLOCK):
            a = _dot(hn, win_ref[:, A0 + c0:A0 + c0 + GLU_BLOCK])
            yield
            b = _dot(hn, win_ref[:, B0 + c0:B0 + c0 + GLU_BLOCK])
            uext_ref[HIST:HIST + TM, c0:c0 + GLU_BLOCK] = a * jax.nn.sigmoid(b)
            yield

    def layer_norm_slab(s, zero):
        rows = slice(s * SLAB, (s + 1) * SLAB)
        uf = _gate_rows(conv_ref[rows, :], zero)
        mu = jnp.mean(uf, axis=-1, keepdims=True)
        var = jnp.mean(jnp.square(uf - mu), axis=-1, keepdims=True)
        uf = (uf - mu) * lax.rsqrt(var + EPS) * clng_ref[...] + clnb_ref[...]
        sw = jax.nn.silu(uf)
        act_ref[rows, :] = sw.astype(BF16)
        return sw[:SUBLANES, :LANES]


    def gate_slab(s, zero):
        rows = slice(s * SLAB, (s + 1) * SLAB)
        gc = proj_ref[rows, PROJ_GC:PROJ_GC + CONV_WIDTH]
        if zero is not None:
            gc = gc + jnp.tile(zero, (SLAB // SUBLANES, CONV_WIDTH // LANES))
        out = (conv_ref[rows, :] + pwb_ref[...]) * jax.nn.silu(gc)
        y_ref[rows, RET_WIDTH:] = out.astype(y_ref.dtype)
        return out[:SUBLANES, :LANES]

    def conv_tail():
        yield from _chained_slabs(TM // SLAB, LN_CHAINS, layer_norm_slab)
        conv_ref[...] = _dot(act_ref[...], pww_ref[...])
        yield from _chained_slabs(TM // SLAB, GATE_CHAINS, gate_slab)

    retention = _retention_tail(i, chunk_decay, proj_ref, cosb_ref, sinb_ref, coso_ref, sino_ref,
                                mask_ref, qdec_ref, kdec_ref, gng_ref, state_ref, y_ref)
    _interleave([(glu_projection(), 1), (conv_tail(), TAIL_SLABS_PER_GLU_DOT),
                 (retention, RETENTION_SLABS_PER_GLU_DOT)])

    @pl.when(i < N_TILES)
    def _tile():
        _tile_body(hn_ref, win_ref, dww_ref, dwb_ref, uext_ref, uph_ref, proj_ref, conv_ref)


def _tile_body(hn_ref, win_ref, dww_ref, dwb_ref, uext_ref, uph_ref, proj_ref, conv_ref):
    n_phase = SUBLANES

    shift = HIST - (CONV_K - 1)
    n_acc = CONV_RB // SUBLANES
    n_iter = TM // CONV_RB
    proj_slice = 4 * RET_WIDTH // n_iter
    gate_slice = CONV_WIDTH // n_iter

    def conv_and_project(t, carry):
        pcol = pl.multiple_of(t * proj_slice, proj_slice)
        proj_ref[:, pl.ds(pcol, proj_slice)] = _dot(hn_ref[...],
                                                    win_ref[:, pl.ds(Q0 + pcol, proj_slice)])
        gcol = pl.multiple_of(t * gate_slice, gate_slice)
        proj_ref[:, pl.ds(PROJ_GC + gcol, gate_slice)] = _dot(
            hn_ref[...], win_ref[:, pl.ds(C0 + gcol, gate_slice)])
        row = pl.multiple_of(t * CONV_RB, CONV_RB)
        window = uext_ref[pl.ds(row, HIST + CONV_RB), :]
        for p in range(1, n_phase):
            uph_ref[p, SUBLANES - p:SUBLANES - p + HIST + CONV_RB, :] = window

        def window_vreg(p, idx, cs):
            if p == 0:
                return uext_ref[pl.ds(row + SUBLANES * idx, SUBLANES), cs]
            return uph_ref[p, SUBLANES * (idx + 1):SUBLANES * (idx + 2), cs]

        tails = [None] * CONV_CHAINS
        for cbi in range(CONV_WIDTH // LANES):
            cs = slice(cbi * LANES, (cbi + 1) * LANES)
            bias = jnp.broadcast_to(dwb_ref[:, cs], (SUBLANES, LANES))
            wts = [jnp.broadcast_to(dww_ref[j:j + 1, cs], (SUBLANES, LANES)) for j in range(CONV_K)]
            out = []
            for m0 in range(0, n_acc, CONV_CHAINS):
                width = min(CONV_CHAINS, n_acc - m0)
                accs = [bias if tails[c] is None else bias + _ordered_zero(tails[c])
                        for c in range(width)]
                loaded = {}
                for j in range(CONV_K):
                    p, a = (j + shift) % n_phase, (j + shift) // n_phase
                    for c in range(width):
                        key = (p, a + m0 + c)
                        if key not in loaded:
                            loaded[key] = window_vreg(p, a + m0 + c, cs)
                        accs[c] = accs[c] + loaded[key] * wts[j]
                tails[:width] = accs
                out.extend(accs)
            conv_ref[pl.ds(row, CONV_RB), cs] = jnp.concatenate(out, axis=0)
        return carry

    lax.fori_loop(0, n_iter, conv_and_project, 0)
    uext_ref[0:HIST, :] = uext_ref[TM:TM + HIST, :]


def _retention_tail(i, chunk_decay, proj_ref, cosp_ref, sinp_ref, coso_ref, sino_ref, mask_ref,
                    qdec_ref, kdec_ref, gng_ref, state_ref, y_ref):
    n_slabs = TM // SLAB
    per_chunk = CHUNK // SLAB
    cb, sb = cosp_ref[0][0:1, :], sinp_ref[0][0:1, :]
    cos = cb * coso_ref[...] - sb * sino_ref[...]
    sin = sb * coso_ref[...] + cb * sino_ref[...]
    twice = lambda t: jnp.concatenate([t, t], axis=-1)

    q_bf, k_bf, qd_bf, kd_bf, v_bf = ([[None] * n_slabs for _ in range(RET_HEADS)] for _ in range(5))

    def rotary_slab(u, zero):
        h, s = divmod(u, n_slabs)
        rows = slice(s * SLAB, (s + 1) * SLAB)
        crow = slice((s % per_chunk) * SLAB, (s % per_chunk + 1) * SLAB)
        q_rot = _rotary(_gate_rows(proj_ref[rows, Q0 + h * HEAD_DIM:Q0 + (h + 1) * HEAD_DIM], zero),
                        cos[rows], sin[rows])
        k_rot = _rotary(_gate_rows(proj_ref[rows, K0 + h * HEAD_DIM:K0 + (h + 1) * HEAD_DIM], zero),
                        cos[rows], sin[rows])
        qd = q_rot * twice(qdec_ref[h, crow, :])
        q_bf[h][s], k_bf[h][s] = q_rot.astype(BF16), k_rot.astype(BF16)
        qd_bf[h][s] = qd.astype(BF16)
        kd_bf[h][s] = (k_rot * twice(kdec_ref[h, crow, :])).astype(BF16)
        v_bf[h][s] = proj_ref[rows, V0 + h * HEAD_DIM:V0 + (h + 1) * HEAD_DIM].astype(BF16)
        return qd[:SUBLANES, :LANES]

    yield from _chained_slabs(RET_HEADS * n_slabs, ROT_CHAINS, rotary_slab)

    for h in range(RET_HEADS):
        hs = slice(h * HEAD_DIM, (h + 1) * HEAD_DIM)
        decay = jnp.where(i == 0, 1.0, chunk_decay[h])
        for c in range(TM // CHUNK):
            chunk = lambda pieces: jnp.concatenate(pieces[h][c * per_chunk:(c + 1) * per_chunk], axis=0)
            qc, kc, qdc, kdc, vc = (chunk(p) for p in (q_bf, k_bf, qd_bf, kd_bf, v_bf))
            scores = _dot_nt(qc, kc) * mask_ref[h]
            inner = _dot(scores.astype(BF16), vc)
            state = state_ref[h]
            cross = _dot(qdc, state.astype(BF16))
            update = _dot_tn(kdc, vc)

            def state_slab(s, zero):
                rows = slice(s * SLAB, (s + 1) * SLAB)
                scale = decay if zero is None else jnp.tile(decay + zero, (SLAB // SUBLANES, 2))
                new = state[rows] * scale + update[rows]
                state_ref[h, rows, :] = new
                return new[:SUBLANES, :LANES]

            yield from _chained_slabs(HEAD_DIM // SLAB, 1, state_slab)
            y = inner + cross

            def norm_slab(s, zero):
                rows = slice(s * SLAB, (s + 1) * SLAB)
                trow = slice(c * CHUNK + s * SLAB, c * CHUNK + (s + 1) * SLAB)
                ys = _gate_rows(y[rows], zero)
                mu = jnp.mean(ys, axis=-1, keepdims=True)
                var = jnp.mean(jnp.square(ys - mu), axis=-1, keepdims=True)
                yn = (ys - mu) * lax.rsqrt(var + EPS) * gng_ref[:, hs]
                gate = proj_ref[trow, G0 + h * HEAD_DIM:G0 + (h + 1) * HEAD_DIM]
                out = yn * jax.nn.silu(gate)
                y_ref[trow, hs] = out.astype(y_ref.dtype)
                return out[:SUBLANES, :LANES]

            yield from _chained_slabs(per_chunk, NORM_CHAINS, norm_slab)


def _out_kernel(y_ref, x_ref, wout_ref, fg_ref, o_ref):
    h = x_ref[...] + _dot(y_ref[...], wout_ref[...])
    o_ref[...] = _rms_norm_rows(h, fg_ref[...])


def _resident(shape):
    nd = len(shape)
    return pl.BlockSpec(shape, lambda i: (0,) * nd, pipeline_mode=pl.Buffered(1))


def kernel(x, meta_tokens, ln_g, w_in, ret_gn_g, conv_dw_w, conv_dw_b, conv_ln_g, conv_ln_b,
           conv_pw_w, conv_pw_b, w_out, final_g):
    assert x.shape == (1, SEQ, D_MODEL) and w_in.shape == (1, D_MODEL, IN_WIDTH)
    tabs, chunk_decay = _constant_tables()
    x2 = x[0]
    row = lambda a: a.reshape(1, -1)
    hbm = pl.BlockSpec(memory_space=pl.ANY)

    tile = lambda i: jnp.minimum(i, N_TILES - 1)
    prev_tile = lambda i: jnp.maximum(i - 1, 0)
    mix_in = [
        (x2, pl.BlockSpec((TM, D_MODEL), lambda i: (tile(i), 0))),
        (meta_tokens, _resident((N_META, D_MODEL))),
        (row(ln_g[0]), _resident((1, D_MODEL))),
        (w_in[0], hbm),
        (tabs["cos_b"], pl.BlockSpec((1, 8, HALF), lambda i: (prev_tile(i), 0, 0))),
        (tabs["sin_b"], pl.BlockSpec((1, 8, HALF), lambda i: (prev_tile(i), 0, 0))),
        (tabs["cos_b"], pl.BlockSpec((1, 8, HALF), lambda i: (N_TILES, 0, 0))),
        (tabs["sin_b"], pl.BlockSpec((1, 8, HALF), lambda i: (N_TILES, 0, 0))),
        (tabs["cos_o"], _resident((TM, HALF))),
        (tabs["sin_o"], _resident((TM, HALF))),
        (tabs["mask"], _resident((RET_HEADS, CHUNK, CHUNK))),
        (tabs["qdec"], _resident((RET_HEADS, CHUNK, LANES))),
        (tabs["kdec"], _resident((RET_HEADS, CHUNK, LANES))),
        (row(ret_gn_g[0]), _resident((1, RET_WIDTH))),
        (conv_dw_w[0], _resident((CONV_K, CONV_WIDTH))),
        (row(conv_dw_b[0]), _resident((1, CONV_WIDTH))),
        (row(conv_ln_g[0]), _resident((1, CONV_WIDTH))),
        (row(conv_ln_b[0]), _resident((1, CONV_WIDTH))),
        (conv_pw_w[0], hbm),
        (row(conv_pw_b[0]), _resident((1, CONV_WIDTH))),
        (w_out[0], pl.BlockSpec((WOUT_ROWS, D_MODEL), lambda i: (tile(i), 0))),
    ]
    y, w_out_bf = pl.pallas_call(
        functools.partial(_mix_kernel, chunk_decay),
        grid=(N_TILES + 1,),
        in_specs=[s for _, s in mix_in],
        out_specs=[pl.BlockSpec((TM, D_MODEL), lambda i: (prev_tile(i), 0)),
                   pl.BlockSpec((WOUT_ROWS, D_MODEL), lambda i: (tile(i), 0))],
        out_shape=[jax.ShapeDtypeStruct((SEQ, D_MODEL), BF16),
                   jax.ShapeDtypeStruct((D_MODEL, D_MODEL), BF16)],
        scratch_shapes=[
            pltpu.VMEM((D_MODEL, IN_WIDTH), BF16),
            pltpu.VMEM((CONV_WIDTH, CONV_WIDTH), BF16),
            pltpu.SemaphoreType.DMA((TM // STAGE_ROWS,)),
            pltpu.VMEM((RET_HEADS, HEAD_DIM, HEAD_DIM), F32),
            pltpu.VMEM((HIST + TM, CONV_WIDTH), F32),
            pltpu.VMEM((SUBLANES, SUBLANES + HIST + CONV_RB, CONV_WIDTH), F32),
            pltpu.VMEM((TM, PROJ_GC + CONV_WIDTH), F32),
            pltpu.VMEM((TM, CONV_WIDTH), F32),
            pltpu.VMEM((TM, D_MODEL), BF16),
            pltpu.VMEM((TM, CONV_WIDTH), BF16),
        ],
        compiler_params=pltpu.CompilerParams(
            dimension_semantics=("arbitrary",), vmem_limit_bytes=VMEM_LIMIT_MIX),
        name="mix",
    )(*[a for a, _ in mix_in])

    out = pl.pallas_call(
        _out_kernel,
        grid=(SEQ // TM_OUT,),
        in_specs=[
            pl.BlockSpec((TM_OUT, D_MODEL), lambda i: (i, 0)),
            pl.BlockSpec((TM_OUT, D_MODEL), lambda i: (i, 0)),
            _resident((D_MODEL, D_MODEL)),
            _resident((1, D_MODEL)),
        ],
        out_specs=pl.BlockSpec((TM_OUT, D_MODEL), lambda i: (i, 0)),
        out_shape=jax.ShapeDtypeStruct((SEQ, D_MODEL), x.dtype),
        compiler_params=pltpu.CompilerParams(
            dimension_semantics=("arbitrary",), vmem_limit_bytes=VMEM_LIMIT_OUT),
        name="out",
    )(y, x2, w_out_bf, row(final_g))
    return out[None]
```

```python
import functools

import numpy as np
import jax
import jax.numpy as jnp
from jax import lax
from jax.experimental import pallas as pl
from jax.experimental.pallas import tpu as pltpu

D_MODEL = 2048
SEQ = 8192
N_META = 16
RET_WIDTH = 1024
RET_HEADS = 4
HEAD_DIM = 256
HALF = HEAD_DIM // 2
CONV_WIDTH = 1024
CONV_K = 31
CHUNK = 128
ROPE_BASE = 10000.0
EPS = 1e-6
IN_WIDTH = 4 * RET_WIDTH + 3 * CONV_WIDTH

Q0, K0, V0, G0, A0, B0, C0 = 0, 1024, 2048, 3072, 4096, 5120, 6144
PROJ_GC = 4 * RET_WIDTH

LANES = 128
SUBLANES = 8
TM = 256
N_TILES = SEQ // TM
RCHUNK = TM
HIST = 32
TM_OUT = 512
CONV_RB = 128
SLAB = 16
STATE_SLAB = 64
LN_CHAINS = 4
GATE_CHAINS = 2
ROT_CHAINS = 2
NORM_CHAINS = 8
GLU_BLOCK = 256
TAIL_SLABS_PER_GLU_DOT = 3
RETENTION_SLABS_PER_GLU_DOT = 24
CONV_CHAINS = 2
STAGE_ROWS = 32
WOUT_ROWS = D_MODEL // N_TILES
VMEM_LIMIT_MIX = 62 * 1024 * 1024
VMEM_LIMIT_OUT = 48 * 1024 * 1024

BF16 = jnp.bfloat16
F32 = jnp.float32


def _constant_tables():
    half = HALF
    inv_freq = ROPE_BASE ** (-np.arange(half, dtype=np.float64) / half)
    bases = np.concatenate([N_META + TM * np.arange(N_TILES, dtype=np.float64),
                            [-(CHUNK - N_META)]])
    ang_b = bases[:, None] * inv_freq[None, :]
    ang_o = np.arange(TM, dtype=np.float64)[:, None] * inv_freq[None, :]
    cos_b = np.broadcast_to(np.cos(ang_b)[:, None, :], (N_TILES + 1, 8, half))
    sin_b = np.broadcast_to(np.sin(ang_b)[:, None, :], (N_TILES + 1, 8, half))
    gamma = 1.0 - np.exp2(-5.0 - np.arange(RET_HEADS, dtype=np.float64))
    log_g = np.log(gamma)
    idx = np.arange(RCHUNK, dtype=np.float64)
    rel = idx[:, None] - idx[None, :]
    scale = HEAD_DIM ** -0.5
    mask = np.where(rel[None] >= 0, np.exp(np.maximum(rel, 0.0)[None] * log_g[:, None, None]), 0.0)
    q_decay = np.exp((idx[None, :] + 1.0) * log_g[:, None])
    k_decay = np.exp((RCHUNK - 1.0 - idx[None, :]) * log_g[:, None])
    chunk_decay = np.exp(RCHUNK * log_g)
    qdec = np.broadcast_to(q_decay[:, :, None], (RET_HEADS, RCHUNK, LANES))
    kdec = np.broadcast_to((k_decay * scale)[:, :, None], (RET_HEADS, RCHUNK, LANES))
    f = lambda a: jnp.asarray(np.ascontiguousarray(a), dtype=F32)
    tabs = dict(cos_b=f(cos_b), sin_b=f(sin_b), cos_o=f(np.cos(ang_o)), sin_o=f(np.sin(ang_o)),
                mask=f(mask * scale), qdec=f(qdec), kdec=f(kdec))
    return tabs, tuple(float(c) for c in chunk_decay)


def _rms_norm_rows(xf, g):
    ms = jnp.mean(xf * xf, axis=-1, keepdims=True)
    return xf * lax.rsqrt(ms + EPS) * g


def _rotary(x, cos, sin):
    x1, x2 = x[:, :HALF], x[:, HALF:]
    return jnp.concatenate([x1 * cos - x2 * sin, x1 * sin + x2 * cos], axis=-1)


def _ordered_zero(v):
    bits = lax.bitcast_convert_type(v, jnp.uint32)
    return lax.bitcast_convert_type((bits >> 16) >> 16, F32)


def _gate_rows(x, zero):
    if zero is None:
        return x
    head = x[:, :LANES] + jnp.concatenate([zero] * (x.shape[0] // SUBLANES), axis=0)
    return jnp.concatenate([head, x[:, LANES:]], axis=1)


def _chained_slabs(n_slabs, n_chains, slab_fn):
    tails = [None] * n_chains
    for s in range(n_slabs):
        c = s % n_chains
        tails[c] = slab_fn(s, None if tails[c] is None else _ordered_zero(tails[c]))
        yield


def _interleave(streams):
    live = list(streams)
    while live:
        for entry in list(live):
            stream, weight = entry
            for _ in range(weight):
                if next(stream, _interleave) is _interleave:
                    live.remove(entry)
                    break


def _dot(a, b):
    return jnp.dot(a, b, preferred_element_type=F32)


def _dot_nt(a, b):
    return lax.dot_general(a, b, (((1,), (1,)), ((), ())), preferred_element_type=F32)


def _dot_tn(a, b):
    return lax.dot_general(a, b, (((0,), (0,)), ((), ())), preferred_element_type=F32)


def _load_cast_weight(w_hbm, w_vmem, stage_ref, sem_ref, rows):
    n_slots = stage_ref.shape[0] // rows
    n_chunks = w_hbm.shape[0] // rows
    assert sem_ref.shape[0] >= n_slots and n_chunks >= n_slots

    def slot_rows(slot):
        return pl.ds(pl.multiple_of(slot * rows, rows), rows)

    def chunk_copy(c, slot):
        return pltpu.make_async_copy(w_hbm.at[pl.ds(c * rows, rows), :],
                                     stage_ref.at[slot_rows(slot), :], sem_ref.at[slot])

    for c in range(n_slots - 1):
        chunk_copy(c, c).start()

    def body(c, carry):
        slot = c % n_slots
        chunk_copy(c, slot).wait()
        nxt = c + n_slots - 1

        @pl.when(nxt < n_chunks)
        def _prefetch():
            chunk_copy(nxt, nxt % n_slots).start()

        r0 = pl.multiple_of(c * rows, rows)
        w_vmem[pl.ds(r0, rows), :] = stage_ref[slot_rows(slot), :].astype(BF16)
        return carry

    lax.fori_loop(0, n_chunks, body, 0)


def _mix_kernel(chunk_decay,
                x_ref, meta_ref, lng_ref, win_hbm, cosb_ref, sinb_ref, cosm_ref, sinm_ref,
                coso_ref, sino_ref, mask_ref, qdec_ref, kdec_ref, gng_ref,
                dww_ref, dwb_ref, clng_ref, clnb_ref, pww_hbm, pwb_ref, wout_rows_ref,
                y_ref, wout_bf_ref,
                win_ref, pww_ref, sem_ref,
                state_ref, uext_ref, uph_ref, proj_ref, conv_ref, hn_ref, act_ref):
    i = pl.program_id(0)

    def meta_state(h, k_rot, v):
        kdec = kdec_ref[h, RCHUNK - CHUNK:RCHUNK, :]
        kd = (k_rot * jnp.concatenate([kdec, kdec], axis=-1)).astype(BF16)
        state_ref[h] = _dot_tn(kd, v.astype(BF16))

    @pl.when(i == 0)
    def _meta():
        stage_w = proj_ref.shape[1]
        for c0 in range(0, IN_WIDTH, stage_w):
            cols = pl.ds(c0, min(stage_w, IN_WIDTH - c0))
            _load_cast_weight(win_hbm.at[:, cols], win_ref.at[:, cols],
                              proj_ref.at[:, pl.ds(0, cols.size)], sem_ref, STAGE_ROWS)
        _load_cast_weight(pww_hbm, pww_ref, proj_ref.at[:, pl.ds(0, CONV_WIDTH)], sem_ref,
                          STAGE_ROWS)
        conv_ref[...] = jnp.zeros_like(conv_ref)
        proj_ref[...] = jnp.zeros_like(proj_ref)
        hn_ref[0:CHUNK - N_META, :] = jnp.zeros((CHUNK - N_META, D_MODEL), BF16)
        hn_ref[CHUNK - N_META:CHUNK, :] = _rms_norm_rows(meta_ref[...], lng_ref[...]).astype(BF16)
        hm = hn_ref[0:CHUNK, :]
        cm, sm = cosm_ref[0][0:1, :], sinm_ref[0][0:1, :]
        cos = cm * coso_ref[:CHUNK] - sm * sino_ref[:CHUNK]
        sin = sm * coso_ref[:CHUNK] + cm * sino_ref[:CHUNK]
        k = _dot(hm, win_ref[:, K0:K0 + RET_WIDTH])
        v = _dot(hm, win_ref[:, V0:V0 + RET_WIDTH])
        for h in range(RET_HEADS):
            hs = slice(h * HEAD_DIM, (h + 1) * HEAD_DIM)
            meta_state(h, _rotary(k[:, hs], cos, sin), v[:, hs])
        a = _dot(hm, win_ref[:, A0:A0 + CONV_WIDTH])
        b = _dot(hm, win_ref[:, B0:B0 + CONV_WIDTH])
        u = a * jax.nn.sigmoid(b)
        uext_ref[0:HIST, :] = u[CHUNK - HIST:, :]

    wout_bf_ref[...] = wout_rows_ref[...].astype(BF16)

    hn_ref[...] = _rms_norm_rows(x_ref[...], lng_ref[...]).astype(BF16)
    hn = hn_ref[...]

    def glu_projection():
        for c0 in range(0, CONV_WIDTH, GLU_BLOCK):
            a = _dot(hn, win_ref[:, A0 + c0:A0 + c0 + GLU_BLOCK])
            yield
            b = _dot(hn, win_ref[:, B0 + c0:B0 + c0 + GLU_BLOCK])
            uext_ref[HIST:HIST + TM, c0:c0 + GLU_BLOCK] = a * jax.nn.sigmoid(b)
            yield

    def layer_norm_slab(s, zero):
        rows = slice(s * SLAB, (s + 1) * SLAB)
        uf = _gate_rows(conv_ref[rows, :], zero)
        mu = jnp.mean(uf, axis=-1, keepdims=True)
        var = jnp.mean(jnp.square(uf - mu), axis=-1, keepdims=True)
        uf = (uf - mu) * lax.rsqrt(var + EPS) * clng_ref[...] + clnb_ref[...]
        sw = jax.nn.silu(uf)
        act_ref[rows, :] = sw.astype(BF16)
        return sw[:SUBLANES, :LANES]


    def gate_slab(s, zero):
        rows = slice(s * SLAB, (s + 1) * SLAB)
        gc = proj_ref[rows, PROJ_GC:PROJ_GC + CONV_WIDTH]
        if zero is not None:
            gc = gc + jnp.tile(zero, (SLAB // SUBLANES, CONV_WIDTH // LANES))
        out = (conv_ref[rows, :] + pwb_ref[...]) * jax.nn.silu(gc)
        y_ref[rows, RET_WIDTH:] = out.astype(y_ref.dtype)
        return out[:SUBLANES, :LANES]

    def conv_tail():
        yield from _chained_slabs(TM // SLAB, LN_CHAINS, layer_norm_slab)
        conv_ref[...] = _dot(act_ref[...], pww_ref[...])
        yield from _chained_slabs(TM // SLAB, GATE_CHAINS, gate_slab)

    retention = _retention_tail(i, chunk_decay, proj_ref, cosb_ref, sinb_ref, coso_ref, sino_ref,
                                mask_ref, qdec_ref, kdec_ref, gng_ref, state_ref, y_ref)
    _interleave([(glu_projection(), 1), (conv_tail(), TAIL_SLABS_PER_GLU_DOT),
                 (retention, RETENTION_SLABS_PER_GLU_DOT)])

    @pl.when(i < N_TILES)
    def _tile():
        _tile_body(hn_ref, win_ref, dww_ref, dwb_ref, uext_ref, uph_ref, proj_ref, conv_ref)


def _tile_body(hn_ref, win_ref, dww_ref, dwb_ref, uext_ref, uph_ref, proj_ref, conv_ref):
    n_phase = SUBLANES

    shift = HIST - (CONV_K - 1)
    n_acc = CONV_RB // SUBLANES
    n_iter = TM // CONV_RB
    proj_slice = 4 * RET_WIDTH // n_iter
    gate_slice = CONV_WIDTH // n_iter

    def conv_and_project(t, carry):
        pcol = pl.multiple_of(t * proj_slice, proj_slice)
        proj_ref[:, pl.ds(pcol, proj_slice)] = _dot(hn_ref[...],
                                                    win_ref[:, pl.ds(Q0 + pcol, proj_slice)])
        gcol = pl.multiple_of(t * gate_slice, gate_slice)
        proj_ref[:, pl.ds(PROJ_GC + gcol, gate_slice)] = _dot(
            hn_ref[...], win_ref[:, pl.ds(C0 + gcol, gate_slice)])
        row = pl.multiple_of(t * CONV_RB, CONV_RB)
        window = uext_ref[pl.ds(row, HIST + CONV_RB), :]
        for p in range(1, n_phase):
            uph_ref[p, SUBLANES - p:SUBLANES - p + HIST + CONV_RB, :] = window

        def window_vreg(p, idx, cs):
            if p == 0:
                return uext_ref[pl.ds(row + SUBLANES * idx, SUBLANES), cs]
            return uph_ref[p, SUBLANES * (idx + 1):SUBLANES * (idx + 2), cs]

        tails = [None] * CONV_CHAINS
        for cbi in range(CONV_WIDTH // LANES):
            cs = slice(cbi * LANES, (cbi + 1) * LANES)
            bias = jnp.broadcast_to(dwb_ref[:, cs], (SUBLANES, LANES))
            wts = [jnp.broadcast_to(dww_ref[j:j + 1, cs], (SUBLANES, LANES)) for j in range(CONV_K)]
            out = []
            for m0 in range(0, n_acc, CONV_CHAINS):
                width = min(CONV_CHAINS, n_acc - m0)
                accs = [bias if tails[c] is None else bias + _ordered_zero(tails[c])
                        for c in range(width)]
                loaded = {}
                for j in range(CONV_K):
                    p, a = (j + shift) % n_phase, (j + shift) // n_phase
                    for c in range(width):
                        key = (p, a + m0 + c)
                        if key not in loaded:
                            loaded[key] = window_vreg(p, a + m0 + c, cs)
                        accs[c] = accs[c] + loaded[key] * wts[j]
                tails[:width] = accs
                out.extend(accs)
            conv_ref[pl.ds(row, CONV_RB), cs] = jnp.concatenate(out, axis=0)
        return carry

    lax.fori_loop(0, n_iter, conv_and_project, 0)
    uext_ref[0:HIST, :] = uext_ref[TM:TM + HIST, :]


def _retention_tail(i, chunk_decay, proj_ref, cosp_ref, sinp_ref, coso_ref, sino_ref, mask_ref,
                    qdec_ref, kdec_ref, gng_ref, state_ref, y_ref):
    n_slabs = TM // SLAB
    per_chunk = RCHUNK // SLAB
    cb, sb = cosp_ref[0][0:1, :], sinp_ref[0][0:1, :]
    cos = cb * coso_ref[...] - sb * sino_ref[...]
    sin = sb * coso_ref[...] + cb * sino_ref[...]
    twice = lambda t: jnp.concatenate([t, t], axis=-1)

    q_bf, k_bf, qd_bf, kd_bf, v_bf = ([[None] * n_slabs for _ in range(RET_HEADS)] for _ in range(5))

    def rotary_slab(u, zero):
        h, s = divmod(u, n_slabs)
        rows = slice(s * SLAB, (s + 1) * SLAB)
        crow = slice((s % per_chunk) * SLAB, (s % per_chunk + 1) * SLAB)
        q_rot = _rotary(_gate_rows(proj_ref[rows, Q0 + h * HEAD_DIM:Q0 + (h + 1) * HEAD_DIM], zero),
                        cos[rows], sin[rows])
        k_rot = _rotary(_gate_rows(proj_ref[rows, K0 + h * HEAD_DIM:K0 + (h + 1) * HEAD_DIM], zero),
                        cos[rows], sin[rows])
        qd = q_rot * twice(qdec_ref[h, crow, :])
        q_bf[h][s], k_bf[h][s] = q_rot.astype(BF16), k_rot.astype(BF16)
        qd_bf[h][s] = qd.astype(BF16)
        kd_bf[h][s] = (k_rot * twice(kdec_ref[h, crow, :])).astype(BF16)
        v_bf[h][s] = proj_ref[rows, V0 + h * HEAD_DIM:V0 + (h + 1) * HEAD_DIM].astype(BF16)
        return qd[:SUBLANES, :LANES]

    yield from _chained_slabs(RET_HEADS * n_slabs, ROT_CHAINS, rotary_slab)

    for h in range(RET_HEADS):
        hs = slice(h * HEAD_DIM, (h + 1) * HEAD_DIM)
        decay = jnp.where(i == 0, 1.0, chunk_decay[h])
        for c in range(TM // RCHUNK):
            chunk = lambda pieces: jnp.concatenate(pieces[h][c * per_chunk:(c + 1) * per_chunk], axis=0)
            qc, kc, qdc, kdc, vc = (chunk(p) for p in (q_bf, k_bf, qd_bf, kd_bf, v_bf))
            scores = _dot_nt(qc, kc) * mask_ref[h]
            inner = _dot(scores.astype(BF16), vc)
            state = state_ref[h]
            cross = _dot(qdc, state.astype(BF16))
            update = _dot_tn(kdc, vc)

            def state_slab(s, zero):
                rows = slice(s * STATE_SLAB, (s + 1) * STATE_SLAB)
                scale = decay if zero is None else jnp.tile(decay + zero, (STATE_SLAB // SUBLANES, 2))
                new = state[rows] * scale + update[rows]
                state_ref[h, rows, :] = new
                return new[:SUBLANES, :LANES]

            yield from _chained_slabs(HEAD_DIM // STATE_SLAB, 1, state_slab)
            y = inner + cross

            def norm_slab(s, zero):
                rows = slice(s * SLAB, (s + 1) * SLAB)
                trow = slice(c * RCHUNK + s * SLAB, c * RCHUNK + (s + 1) * SLAB)
                ys = _gate_rows(y[rows], zero)
                mu = jnp.mean(ys, axis=-1, keepdims=True)
                var = jnp.mean(jnp.square(ys - mu), axis=-1, keepdims=True)
                yn = (ys - mu) * lax.rsqrt(var + EPS) * gng_ref[:, hs]
                gate = proj_ref[trow, G0 + h * HEAD_DIM:G0 + (h + 1) * HEAD_DIM]
                out = yn * jax.nn.silu(gate)
                y_ref[trow, hs] = out.astype(y_ref.dtype)
                return out[:SUBLANES, :LANES]

            yield from _chained_slabs(per_chunk, NORM_CHAINS, norm_slab)


def _out_kernel(y_ref, x_ref, wout_ref, fg_ref, o_ref):
    h = x_ref[...] + _dot(y_ref[...], wout_ref[...])
    o_ref[...] = _rms_norm_rows(h, fg_ref[...])


def _resident(shape):
    nd = len(shape)
    return pl.BlockSpec(shape, lambda i: (0,) * nd, pipeline_mode=pl.Buffered(1))


def kernel(x, meta_tokens, ln_g, w_in, ret_gn_g, conv_dw_w, conv_dw_b, conv_ln_g, conv_ln_b,
           conv_pw_w, conv_pw_b, w_out, final_g):
    assert x.shape == (1, SEQ, D_MODEL) and w_in.shape == (1, D_MODEL, IN_WIDTH)
    tabs, chunk_decay = _constant_tables()
    x2 = x[0]
    row = lambda a: a.reshape(1, -1)
    hbm = pl.BlockSpec(memory_space=pl.ANY)

    tile = lambda i: jnp.minimum(i, N_TILES - 1)
    prev_tile = lambda i: jnp.maximum(i - 1, 0)
    mix_in = [
        (x2, pl.BlockSpec((TM, D_MODEL), lambda i: (tile(i), 0))),
        (meta_tokens, _resident((N_META, D_MODEL))),
        (row(ln_g[0]), _resident((1, D_MODEL))),
        (w_in[0], hbm),
        (tabs["cos_b"], pl.BlockSpec((1, 8, HALF), lambda i: (prev_tile(i), 0, 0))),
        (tabs["sin_b"], pl.BlockSpec((1, 8, HALF), lambda i: (prev_tile(i), 0, 0))),
        (tabs["cos_b"], pl.BlockSpec((1, 8, HALF), lambda i: (N_TILES, 0, 0))),
        (tabs["sin_b"], pl.BlockSpec((1, 8, HALF), lambda i: (N_TILES, 0, 0))),
        (tabs["cos_o"], _resident((TM, HALF))),
        (tabs["sin_o"], _resident((TM, HALF))),
        (tabs["mask"], _resident((RET_HEADS, RCHUNK, RCHUNK))),
        (tabs["qdec"], _resident((RET_HEADS, RCHUNK, LANES))),
        (tabs["kdec"], _resident((RET_HEADS, RCHUNK, LANES))),
        (row(ret_gn_g[0]), _resident((1, RET_WIDTH))),
        (conv_dw_w[0], _resident((CONV_K, CONV_WIDTH))),
        (row(conv_dw_b[0]), _resident((1, CONV_WIDTH))),
        (row(conv_ln_g[0]), _resident((1, CONV_WIDTH))),
        (row(conv_ln_b[0]), _resident((1, CONV_WIDTH))),
        (conv_pw_w[0], hbm),
        (row(conv_pw_b[0]), _resident((1, CONV_WIDTH))),
        (w_out[0], pl.BlockSpec((WOUT_ROWS, D_MODEL), lambda i: (tile(i), 0))),
    ]
    y, w_out_bf = pl.pallas_call(
        functools.partial(_mix_kernel, chunk_decay),
        grid=(N_TILES + 1,),
        in_specs=[s for _, s in mix_in],
        out_specs=[pl.BlockSpec((TM, D_MODEL), lambda i: (prev_tile(i), 0)),
                   pl.BlockSpec((WOUT_ROWS, D_MODEL), lambda i: (tile(i), 0))],
        out_shape=[jax.ShapeDtypeStruct((SEQ, D_MODEL), BF16),
                   jax.ShapeDtypeStruct((D_MODEL, D_MODEL), BF16)],
        scratch_shapes=[
            pltpu.VMEM((D_MODEL, IN_WIDTH), BF16),
            pltpu.VMEM((CONV_WIDTH, CONV_WIDTH), BF16),
            pltpu.SemaphoreType.DMA((TM // STAGE_ROWS,)),
            pltpu.VMEM((RET_HEADS, HEAD_DIM, HEAD_DIM), F32),
            pltpu.VMEM((HIST + TM, CONV_WIDTH), F32),
            pltpu.VMEM((SUBLANES, SUBLANES + HIST + CONV_RB, CONV_WIDTH), F32),
            pltpu.VMEM((TM, PROJ_GC + CONV_WIDTH), F32),
            pltpu.VMEM((TM, CONV_WIDTH), F32),
            pltpu.VMEM((TM, D_MODEL), BF16),
            pltpu.VMEM((TM, CONV_WIDTH), BF16),
        ],
        compiler_params=pltpu.CompilerParams(
            dimension_semantics=("arbitrary",), vmem_limit_bytes=VMEM_LIMIT_MIX),
        name="mix",
    )(*[a for a, _ in mix_in])

    out = pl.pallas_call(
        _out_kernel,
        grid=(SEQ // TM_OUT,),
        in_specs=[
            pl.BlockSpec((TM_OUT, D_MODEL), lambda i: (i, 0)),
            pl.BlockSpec((TM_OUT, D_MODEL), lambda i: (i, 0)),
            _resident((D_MODEL, D_MODEL)),
            _resident((1, D_MODEL)),
        ],
        out_specs=pl.BlockSpec((TM_OUT, D_MODEL), lambda i: (i, 0)),
        out_shape=jax.ShapeDtypeStruct((SEQ, D_MODEL), x.dtype),
        compiler_params=pltpu.CompilerParams(
            dimension_semantics=("arbitrary",), vmem_limit_bytes=VMEM_LIMIT_OUT),
        name="out",
    )(y, x2, w_out_bf, row(final_g))
    return out[None]
```

```python
import functools

import numpy as np
import jax
import jax.numpy as jnp
from jax import lax
from jax.experimental import pallas as pl
from jax.experimental.pallas import tpu as pltpu

D_MODEL = 2048
SEQ = 8192
N_META = 16
RET_WIDTH = 1024
RET_HEADS = 4
HEAD_DIM = 256
HALF = HEAD_DIM // 2
CONV_WIDTH = 1024
CONV_K = 31
CHUNK = 128
ROPE_BASE = 10000.0
EPS = 1e-6
IN_WIDTH = 4 * RET_WIDTH + 3 * CONV_WIDTH

Q0, K0, V0, G0, A0, B0, C0 = 0, 1024, 2048, 3072, 4096, 5120, 6144
PROJ_GC = 4 * RET_WIDTH

LANES = 128
SUBLANES = 8
TM = 256
N_TILES = SEQ // TM
RCHUNK = TM
HIST = 32
TM_OUT = 512
CONV_CHAINS = 2
SLAB = 16
STATE_SLAB = 64
LN_CHAINS = 4
GLU_BLOCK = 256
TAIL_SLABS_PER_GLU_DOT = 3
RETENTION_SLABS_PER_GLU_DOT = 32
STAGE_ROWS = 32
WOUT_ROWS = D_MODEL // N_TILES
VMEM_CAPACITY = 64 * 1024 * 1024
VMEM_LIMIT_MIX = VMEM_CAPACITY - 1024 * 1024
VMEM_LIMIT_OUT = 48 * 1024 * 1024

BF16 = jnp.bfloat16
F32 = jnp.float32


def _constant_tables():
    half = HALF
    inv_freq = ROPE_BASE ** (-np.arange(half, dtype=np.float64) / half)
    bases = np.concatenate([N_META + TM * np.arange(N_TILES, dtype=np.float64),
                            [-(CHUNK - N_META)]])
    ang_b = bases[:, None] * inv_freq[None, :]
    ang_o = np.arange(TM, dtype=np.float64)[:, None] * inv_freq[None, :]
    cos_b = np.broadcast_to(np.cos(ang_b)[:, None, :], (N_TILES + 1, 8, half))
    sin_b = np.broadcast_to(np.sin(ang_b)[:, None, :], (N_TILES + 1, 8, half))
    gamma = 1.0 - np.exp2(-5.0 - np.arange(RET_HEADS, dtype=np.float64))
    log_g = np.log(gamma)
    idx = np.arange(RCHUNK, dtype=np.float64)
    rel = idx[:, None] - idx[None, :]
    scale = HEAD_DIM ** -0.5
    mask = np.where(rel[None] >= 0, np.exp(np.maximum(rel, 0.0)[None] * log_g[:, None, None]), 0.0)
    q_decay = np.exp((idx[None, :] + 1.0) * log_g[:, None])
    k_decay = np.exp((RCHUNK - 1.0 - idx[None, :]) * log_g[:, None])
    chunk_decay = np.exp(RCHUNK * log_g)
    qdec = np.broadcast_to(q_decay[:, :, None], (RET_HEADS, RCHUNK, LANES))
    kdec = np.broadcast_to((k_decay * scale)[:, :, None], (RET_HEADS, RCHUNK, LANES))
    f = lambda a: jnp.asarray(np.ascontiguousarray(a), dtype=F32)
    tabs = dict(cos_b=f(cos_b), sin_b=f(sin_b), cos_o=f(np.cos(ang_o)), sin_o=f(np.sin(ang_o)),
                mask=f(mask * scale), qdec=f(qdec), kdec=f(kdec))
    return tabs, tuple(float(c) for c in chunk_decay)


def _rms_norm_rows(xf, g):
    ms = jnp.mean(xf * xf, axis=-1, keepdims=True)
    return xf * lax.rsqrt(ms + EPS) * g


def _rotary(x, cos, sin):
    x1, x2 = x[:, :HALF], x[:, HALF:]
    return jnp.concatenate([x1 * cos - x2 * sin, x1 * sin + x2 * cos], axis=-1)


def _ordered_zero(v):
    bits = lax.bitcast_convert_type(v, jnp.uint32)
    return lax.bitcast_convert_type((bits >> 16) >> 16, F32)


def _gate_rows(x, zero):
    if zero is None:
        return x
    head = x[:, :LANES] + jnp.concatenate([zero] * (x.shape[0] // SUBLANES), axis=0)
    return jnp.concatenate([head, x[:, LANES:]], axis=1)


def _chained_slabs(n_slabs, n_chains, slab_fn):
    tails = [None] * n_chains
    for s in range(n_slabs):
        c = s % n_chains
        tails[c] = slab_fn(s, None if tails[c] is None else _ordered_zero(tails[c]))
        yield


def _slabs(n_slabs, slab_fn):
    for s in range(n_slabs):
        slab_fn(s)
        yield


def _interleave(streams):
    live = list(streams)
    while live:
        for entry in list(live):
            stream, weight = entry
            for _ in range(weight):
                if next(stream, _interleave) is _interleave:
                    live.remove(entry)
                    break


def _dot(a, b):
    return jnp.dot(a, b, preferred_element_type=F32)


def _dot_nt(a, b):
    return lax.dot_general(a, b, (((1,), (1,)), ((), ())), preferred_element_type=F32)


def _dot_tn(a, b):
    return lax.dot_general(a, b, (((0,), (0,)), ((), ())), preferred_element_type=F32)


def _load_cast_weight(w_hbm, w_vmem, stage_ref, sem_ref, rows):
    n_slots = stage_ref.shape[0] // rows
    n_chunks = w_hbm.shape[0] // rows
    assert sem_ref.shape[0] >= n_slots and n_chunks >= n_slots

    def slot_rows(slot):
        return pl.ds(pl.multiple_of(slot * rows, rows), rows)

    def chunk_copy(c, slot):
        return pltpu.make_async_copy(w_hbm.at[pl.ds(c * rows, rows), :],
                                     stage_ref.at[slot_rows(slot), :], sem_ref.at[slot])

    for c in range(n_slots - 1):
        chunk_copy(c, c).start()

    def body(c, carry):
        slot = c % n_slots
        chunk_copy(c, slot).wait()
        nxt = c + n_slots - 1

        @pl.when(nxt < n_chunks)
        def _prefetch():
            chunk_copy(nxt, nxt % n_slots).start()

        r0 = pl.multiple_of(c * rows, rows)
        w_vmem[pl.ds(r0, rows), :] = stage_ref[slot_rows(slot), :].astype(BF16)
        return carry

    lax.fori_loop(0, n_chunks, body, 0)


def _mix_kernel(chunk_decay,
                x_ref, meta_ref, lng_ref, win_hbm, cosb_ref, sinb_ref, cosm_ref, sinm_ref,
                coso_ref, sino_ref, mask_ref, qdec_ref, kdec_ref, gng_ref,
                dww_ref, dwb_ref, clng_ref, clnb_ref, pww_hbm, pwb_ref, wout_rows_ref,
                y_ref, wout_bf_ref,
                win_ref, pww_ref, sem_ref,
                state_ref, uext_ref, uph_ref, proj_ref, conv_ref, hn_ref, act_ref):
    i = pl.program_id(0)

    def meta_state(h, k_rot, v):
        kdec = kdec_ref[h, RCHUNK - CHUNK:RCHUNK, :]
        kd = (k_rot * jnp.concatenate([kdec, kdec], axis=-1)).astype(BF16)
        state_ref[h] = _dot_tn(kd, v.astype(BF16))

    @pl.when(i == 0)
    def _meta():
        stage_w = proj_ref.shape[1]
        for c0 in range(0, IN_WIDTH, stage_w):
            cols = pl.ds(c0, min(stage_w, IN_WIDTH - c0))
            _load_cast_weight(win_hbm.at[:, cols], win_ref.at[:, cols],
                              proj_ref.at[:, pl.ds(0, cols.size)], sem_ref, STAGE_ROWS)
        _load_cast_weight(pww_hbm, pww_ref, proj_ref.at[:, pl.ds(0, CONV_WIDTH)], sem_ref,
                          STAGE_ROWS)
        hn_ref[0:CHUNK - N_META, :] = jnp.zeros((CHUNK - N_META, D_MODEL), BF16)
        hn_ref[CHUNK - N_META:CHUNK, :] = _rms_norm_rows(meta_ref[...], lng_ref[...]).astype(BF16)
        hm = hn_ref[0:CHUNK, :]
        cm, sm = cosm_ref[0][0:1, :], sinm_ref[0][0:1, :]
        cos = cm * coso_ref[:CHUNK] - sm * sino_ref[:CHUNK]
        sin = sm * coso_ref[:CHUNK] + cm * sino_ref[:CHUNK]
        k = _dot(hm, win_ref[:, K0:K0 + RET_WIDTH])
        v = _dot(hm, win_ref[:, V0:V0 + RET_WIDTH])
        for h in range(RET_HEADS):
            hs = slice(h * HEAD_DIM, (h + 1) * HEAD_DIM)
            meta_state(h, _rotary(k[:, hs], cos, sin), v[:, hs])
        a = _dot(hm, win_ref[:, A0:A0 + CONV_WIDTH])
        b = _dot(hm, win_ref[:, B0:B0 + CONV_WIDTH])
        u = a * jax.nn.sigmoid(b)
        uext_ref[0:HIST, :] = u[CHUNK - HIST:, :]

    wout_bf_ref[...] = wout_rows_ref[...].astype(BF16)

    def glu_projection():
        hn_ref[...] = _rms_norm_rows(x_ref[...], lng_ref[...]).astype(BF16)
        hn = hn_ref[...]
        for c0 in range(0, CONV_WIDTH, GLU_BLOCK):
            a = _dot(hn, win_ref[:, A0 + c0:A0 + c0 + GLU_BLOCK])
            yield
            b = _dot(hn, win_ref[:, B0 + c0:B0 + c0 + GLU_BLOCK])
            uext_ref[HIST:HIST + TM, c0:c0 + GLU_BLOCK] = a * jax.nn.sigmoid(b)
            yield

    def layer_norm_slab(s, zero):
        rows = slice(s * SLAB, (s + 1) * SLAB)
        uf = _gate_rows(conv_ref[rows, :], zero)
        mu = jnp.mean(uf, axis=-1, keepdims=True)
        var = jnp.mean(jnp.square(uf - mu), axis=-1, keepdims=True)
        uf = (uf - mu) * lax.rsqrt(var + EPS) * clng_ref[...] + clnb_ref[...]
        sw = jax.nn.silu(uf)
        act_ref[rows, :] = sw.astype(BF16)
        return sw[:SUBLANES, :LANES]


    def gate_slab(s):
        rows = slice(s * SLAB, (s + 1) * SLAB)
        gc = proj_ref[rows, PROJ_GC:PROJ_GC + CONV_WIDTH]
        out = (conv_ref[rows, :] + pwb_ref[...]) * jax.nn.silu(gc)
        y_ref[rows, RET_WIDTH:] = out.astype(y_ref.dtype)

    def conv_tail():
        yield from _chained_slabs(TM // SLAB, LN_CHAINS, layer_norm_slab)
        conv_ref[...] = _dot(act_ref[...], pww_ref[...])
        yield from _slabs(TM // SLAB, gate_slab)

    def deferred_streams():
        retention = _retention_tail(chunk_decay, proj_ref, cosb_ref, sinb_ref, coso_ref, sino_ref,
                                    mask_ref, qdec_ref, kdec_ref, gng_ref, state_ref, y_ref)
        return [(conv_tail(), TAIL_SLABS_PER_GLU_DOT), (retention, RETENTION_SLABS_PER_GLU_DOT)]

    @pl.when(i == 0)
    def _first():
        _interleave([(glu_projection(), 1)])

    @pl.when(jnp.logical_and(i > 0, i < N_TILES))
    def _steady():
        _interleave([(glu_projection(), 1)] + deferred_streams())

    @pl.when(i == N_TILES)
    def _last():
        _interleave(deferred_streams())

    @pl.when(i < N_TILES)
    def _tile():
        _tile_body(hn_ref, win_ref, dww_ref, dwb_ref, uext_ref, uph_ref, proj_ref, conv_ref)


def _tile_body(hn_ref, win_ref, dww_ref, dwb_ref, uext_ref, uph_ref, proj_ref, conv_ref):
    proj_ref[:, 0:PROJ_GC] = _dot(hn_ref[...], win_ref[:, Q0:Q0 + PROJ_GC])
    proj_ref[:, PROJ_GC:] = _dot(hn_ref[...], win_ref[:, C0:C0 + CONV_WIDTH])

    n_phase = SUBLANES
    shift = HIST - (CONV_K - 1)
    n_acc = TM // SUBLANES
    window = uext_ref[...]
    for p in range(1, n_phase):
        uph_ref[p - 1, SUBLANES - p:SUBLANES - p + HIST + TM, :] = window

    def window_vreg(p, idx, cs):
        if p == 0:
            return uext_ref[SUBLANES * idx:SUBLANES * (idx + 1), cs]
        return uph_ref[p - 1, SUBLANES * (idx + 1):SUBLANES * (idx + 2), cs]

    tails = [None] * CONV_CHAINS
    for cbi in range(CONV_WIDTH // LANES):
        cs = slice(cbi * LANES, (cbi + 1) * LANES)
        bias = jnp.broadcast_to(dwb_ref[:, cs], (SUBLANES, LANES))
        wts = [jnp.broadcast_to(dww_ref[j:j + 1, cs], (SUBLANES, LANES)) for j in range(CONV_K)]
        out = []
        for m0 in range(0, n_acc, CONV_CHAINS):
            width = min(CONV_CHAINS, n_acc - m0)
            accs = [bias if tails[c] is None else bias + _ordered_zero(tails[c])
                    for c in range(width)]
            loaded = {}
            for j in range(CONV_K):
                p, a = (j + shift) % n_phase, (j + shift) // n_phase
                for c in range(width):
                    key = (p, a + m0 + c)
                    if key not in loaded:
                        loaded[key] = window_vreg(p, a + m0 + c, cs)
                    accs[c] = accs[c] + loaded[key] * wts[j]
            tails[:width] = accs
            out.extend(accs)
        conv_ref[:, cs] = jnp.concatenate(out, axis=0)
    uext_ref[0:HIST, :] = uext_ref[TM:TM + HIST, :]


def _retention_tail(chunk_decay, proj_ref, cosp_ref, sinp_ref, coso_ref, sino_ref, mask_ref,
                    qdec_ref, kdec_ref, gng_ref, state_ref, y_ref):
    n_slabs = TM // SLAB
    per_chunk = RCHUNK // SLAB
    cb, sb = cosp_ref[0][0:1, :], sinp_ref[0][0:1, :]
    cos = cb * coso_ref[...] - sb * sino_ref[...]
    sin = sb * coso_ref[...] + cb * sino_ref[...]
    twice = lambda t: jnp.concatenate([t, t], axis=-1)

    q_bf, k_bf, qd_bf, kd_bf, v_bf = ([[None] * n_slabs for _ in range(RET_HEADS)] for _ in range(5))

    def rotary_slab(u):
        h, s = divmod(u, n_slabs)
        rows = slice(s * SLAB, (s + 1) * SLAB)
        crow = slice((s % per_chunk) * SLAB, (s % per_chunk + 1) * SLAB)
        q_rot = _rotary(proj_ref[rows, Q0 + h * HEAD_DIM:Q0 + (h + 1) * HEAD_DIM], cos[rows], sin[rows])
        k_rot = _rotary(proj_ref[rows, K0 + h * HEAD_DIM:K0 + (h + 1) * HEAD_DIM], cos[rows], sin[rows])
        q_bf[h][s], k_bf[h][s] = q_rot.astype(BF16), k_rot.astype(BF16)
        qd_bf[h][s] = (q_rot * twice(qdec_ref[h, crow, :])).astype(BF16)
        kd_bf[h][s] = (k_rot * twice(kdec_ref[h, crow, :])).astype(BF16)
        v_bf[h][s] = proj_ref[rows, V0 + h * HEAD_DIM:V0 + (h + 1) * HEAD_DIM].astype(BF16)

    yield from _slabs(RET_HEADS * n_slabs, rotary_slab)

    for h in range(RET_HEADS):
        hs = slice(h * HEAD_DIM, (h + 1) * HEAD_DIM)
        decay = chunk_decay[h]
        for c in range(TM // RCHUNK):
            chunk = lambda pieces: jnp.concatenate(pieces[h][c * per_chunk:(c + 1) * per_chunk], axis=0)
            qc, kc, qdc, kdc, vc = (chunk(p) for p in (q_bf, k_bf, qd_bf, kd_bf, v_bf))
            scores = _dot_nt(qc, kc) * mask_ref[h]
            inner = _dot(scores.astype(BF16), vc)
            state = state_ref[h]
            cross = _dot(qdc, state.astype(BF16))
            update = _dot_tn(kdc, vc)

            def state_slab(s):
                rows = slice(s * STATE_SLAB, (s + 1) * STATE_SLAB)
                state_ref[h, rows, :] = state[rows] * decay + update[rows]

            yield from _slabs(HEAD_DIM // STATE_SLAB, state_slab)
            y = inner + cross

            def norm_slab(s):
                rows = slice(s * SLAB, (s + 1) * SLAB)
                trow = slice(c * RCHUNK + s * SLAB, c * RCHUNK + (s + 1) * SLAB)
                ys = y[rows]
                mu = jnp.mean(ys, axis=-1, keepdims=True)
                var = jnp.mean(jnp.square(ys - mu), axis=-1, keepdims=True)
                yn = (ys - mu) * lax.rsqrt(var + EPS) * gng_ref[:, hs]
                gate = proj_ref[trow, G0 + h * HEAD_DIM:G0 + (h + 1) * HEAD_DIM]
                y_ref[trow, hs] = (yn * jax.nn.silu(gate)).astype(y_ref.dtype)

            yield from _slabs(per_chunk, norm_slab)


def _out_kernel(y_ref, x_ref, wout_ref, fg_ref, o_ref):
    h = x_ref[...] + _dot(y_ref[...], wout_ref[...])
    o_ref[...] = _rms_norm_rows(h, fg_ref[...])


def _resident(shape):
    nd = len(shape)
    return pl.BlockSpec(shape, lambda i: (0,) * nd, pipeline_mode=pl.Buffered(1))


def kernel(x, meta_tokens, ln_g, w_in, ret_gn_g, conv_dw_w, conv_dw_b, conv_ln_g, conv_ln_b,
           conv_pw_w, conv_pw_b, w_out, final_g):
    assert x.shape == (1, SEQ, D_MODEL) and w_in.shape == (1, D_MODEL, IN_WIDTH)
    tabs, chunk_decay = _constant_tables()
    x2 = x[0]
    row = lambda a: a.reshape(1, -1)
    hbm = pl.BlockSpec(memory_space=pl.ANY)

    tile = lambda i: jnp.minimum(i, N_TILES - 1)
    prev_tile = lambda i: jnp.maximum(i - 1, 0)
    mix_in = [
        (x2, pl.BlockSpec((TM, D_MODEL), lambda i: (tile(i), 0))),
        (meta_tokens, _resident((N_META, D_MODEL))),
        (row(ln_g[0]), _resident((1, D_MODEL))),
        (w_in[0], hbm),
        (tabs["cos_b"], pl.BlockSpec((1, 8, HALF), lambda i: (prev_tile(i), 0, 0))),
        (tabs["sin_b"], pl.BlockSpec((1, 8, HALF), lambda i: (prev_tile(i), 0, 0))),
        (tabs["cos_b"], pl.BlockSpec((1, 8, HALF), lambda i: (N_TILES, 0, 0))),
        (tabs["sin_b"], pl.BlockSpec((1, 8, HALF), lambda i: (N_TILES, 0, 0))),
        (tabs["cos_o"], _resident((TM, HALF))),
        (tabs["sin_o"], _resident((TM, HALF))),
        (tabs["mask"], _resident((RET_HEADS, RCHUNK, RCHUNK))),
        (tabs["qdec"], _resident((RET_HEADS, RCHUNK, LANES))),
        (tabs["kdec"], _resident((RET_HEADS, RCHUNK, LANES))),
        (row(ret_gn_g[0]), _resident((1, RET_WIDTH))),
        (conv_dw_w[0], _resident((CONV_K, CONV_WIDTH))),
        (row(conv_dw_b[0]), _resident((1, CONV_WIDTH))),
        (row(conv_ln_g[0]), _resident((1, CONV_WIDTH))),
        (row(conv_ln_b[0]), _resident((1, CONV_WIDTH))),
        (conv_pw_w[0], hbm),
        (row(conv_pw_b[0]), _resident((1, CONV_WIDTH))),
        (w_out[0], pl.BlockSpec((WOUT_ROWS, D_MODEL), lambda i: (tile(i), 0))),
    ]
    y, w_out_bf = pl.pallas_call(
        functools.partial(_mix_kernel, chunk_decay),
        grid=(N_TILES + 1,),
        in_specs=[s for _, s in mix_in],
        out_specs=[pl.BlockSpec((TM, D_MODEL), lambda i: (prev_tile(i), 0)),
                   pl.BlockSpec((WOUT_ROWS, D_MODEL), lambda i: (tile(i), 0))],
        out_shape=[jax.ShapeDtypeStruct((SEQ, D_MODEL), BF16),
                   jax.ShapeDtypeStruct((D_MODEL, D_MODEL), BF16)],
        scratch_shapes=[
            pltpu.VMEM((D_MODEL, IN_WIDTH), BF16),
            pltpu.VMEM((CONV_WIDTH, CONV_WIDTH), BF16),
            pltpu.SemaphoreType.DMA((TM // STAGE_ROWS,)),
            pltpu.VMEM((RET_HEADS, HEAD_DIM, HEAD_DIM), F32),
            pltpu.VMEM((HIST + TM, CONV_WIDTH), F32),
            pltpu.VMEM((SUBLANES - 1, SUBLANES + HIST + TM, CONV_WIDTH), F32),
            pltpu.VMEM((TM, PROJ_GC + CONV_WIDTH), F32),
            pltpu.VMEM((TM, CONV_WIDTH), F32),
            pltpu.VMEM((TM, D_MODEL), BF16),
            pltpu.VMEM((TM, CONV_WIDTH), BF16),
        ],
        compiler_params=pltpu.CompilerParams(
            dimension_semantics=("arbitrary",), vmem_limit_bytes=VMEM_LIMIT_MIX),
        name="mix",
    )(*[a for a, _ in mix_in])

    out = pl.pallas_call(
        _out_kernel,
        grid=(SEQ // TM_OUT,),
        in_specs=[
            pl.BlockSpec((TM_OUT, D_MODEL), lambda i: (i, 0)),
            pl.BlockSpec((TM_OUT, D_MODEL), lambda i: (i, 0)),
            _resident((D_MODEL, D_MODEL)),
            _resident((1, D_MODEL)),
        ],
        out_specs=pl.BlockSpec((TM_OUT, D_MODEL), lambda i: (i, 0)),
        out_shape=jax.ShapeDtypeStruct((SEQ, D_MODEL), x.dtype),
        compiler_params=pltpu.CompilerParams(
            dimension_semantics=("arbitrary",), vmem_limit_bytes=VMEM_LIMIT_OUT),
        name="out",
    )(y, x2, w_out_bf, row(final_g))
    return out[None]
```

```python
import functools

import numpy as np
import jax
import jax.numpy as jnp
from jax import lax
from jax.experimental import pallas as pl
from jax.experimental.pallas import tpu as pltpu

D_MODEL = 2048
SEQ = 8192
N_META = 16
RET_WIDTH = 1024
RET_HEADS = 4
HEAD_DIM = 256
HALF = HEAD_DIM // 2
CONV_WIDTH = 1024
CONV_K = 31
CHUNK = 128
ROPE_BASE = 10000.0
EPS = 1e-6
IN_WIDTH = 4 * RET_WIDTH + 3 * CONV_WIDTH

Q0, K0, V0, G0, A0, B0, C0 = 0, 1024, 2048, 3072, 4096, 5120, 6144
PROJ_GC = 4 * RET_WIDTH

LANES = 128
SUBLANES = 8
TM = 256
N_TILES = SEQ // TM
RCHUNK = TM
HIST = 32
TM_OUT = 512
CONV_CHAINS = 2
SLAB = 16
STATE_SLAB = 64
LN_CHAINS = 4
GLU_BLOCK = 256
TAIL_SLABS_PER_GLU_DOT = 3
RETENTION_SLABS_PER_GLU_DOT = 32
STAGE_ROWS = 16
WOUT_ROWS = D_MODEL // N_TILES
VMEM_CAPACITY = 64 * 1024 * 1024
VMEM_LIMIT_MIX = VMEM_CAPACITY - 1024 * 1024
VMEM_LIMIT_OUT = 48 * 1024 * 1024

BF16 = jnp.bfloat16
F32 = jnp.float32


def _constant_tables():
    half = HALF
    inv_freq = ROPE_BASE ** (-np.arange(half, dtype=np.float64) / half)
    bases = np.concatenate([N_META + TM * np.arange(N_TILES, dtype=np.float64),
                            [-(CHUNK - N_META)]])
    ang_b = bases[:, None] * inv_freq[None, :]
    ang_o = np.arange(TM, dtype=np.float64)[:, None] * inv_freq[None, :]
    cos_b = np.broadcast_to(np.cos(ang_b)[:, None, :], (N_TILES + 1, 8, half))
    sin_b = np.broadcast_to(np.sin(ang_b)[:, None, :], (N_TILES + 1, 8, half))
    gamma = 1.0 - np.exp2(-5.0 - np.arange(RET_HEADS, dtype=np.float64))
    log_g = np.log(gamma)
    idx = np.arange(RCHUNK, dtype=np.float64)
    rel = idx[:, None] - idx[None, :]
    scale = HEAD_DIM ** -0.5
    mask = np.where(rel[None] >= 0, np.exp(np.maximum(rel, 0.0)[None] * log_g[:, None, None]), 0.0)
    q_decay = np.exp((idx[None, :] + 1.0) * log_g[:, None])
    k_decay = np.exp((RCHUNK - 1.0 - idx[None, :]) * log_g[:, None])
    chunk_decay = np.exp(RCHUNK * log_g)
    qdec = np.broadcast_to(q_decay[:, :, None], (RET_HEADS, RCHUNK, LANES))
    kdec = np.broadcast_to((k_decay * scale)[:, :, None], (RET_HEADS, RCHUNK, LANES))
    f = lambda a: jnp.asarray(np.ascontiguousarray(a), dtype=F32)
    tabs = dict(cos_b=f(cos_b), sin_b=f(sin_b), cos_o=f(np.cos(ang_o)), sin_o=f(np.sin(ang_o)),
                mask=f(mask * scale), qdec=f(qdec), kdec=f(kdec))
    return tabs, tuple(float(c) for c in chunk_decay)


def _rms_norm_rows(xf, g):
    ms = jnp.mean(xf * xf, axis=-1, keepdims=True)
    return xf * lax.rsqrt(ms + EPS) * g


def _rotary(x, cos, sin):
    x1, x2 = x[:, :HALF], x[:, HALF:]
    return jnp.concatenate([x1 * cos - x2 * sin, x1 * sin + x2 * cos], axis=-1)


def _ordered_zero(v):
    bits = lax.bitcast_convert_type(v, jnp.uint32)
    return lax.bitcast_convert_type((bits >> 16) >> 16, F32)


def _gate_rows(x, zero):
    if zero is None:
        return x
    head = x[:, :LANES] + jnp.concatenate([zero] * (x.shape[0] // SUBLANES), axis=0)
    return jnp.concatenate([head, x[:, LANES:]], axis=1)


def _chained_slabs(n_slabs, n_chains, slab_fn):
    tails = [None] * n_chains
    for s in range(n_slabs):
        c = s % n_chains
        tails[c] = slab_fn(s, None if tails[c] is None else _ordered_zero(tails[c]))
        yield


def _slabs(n_slabs, slab_fn):
    for s in range(n_slabs):
        slab_fn(s)
        yield


def _interleave(streams):
    live = list(streams)
    while live:
        for entry in list(live):
            stream, weight = entry
            for _ in range(weight):
                if next(stream, _interleave) is _interleave:
                    live.remove(entry)
                    break


def _dot(a, b):
    return jnp.dot(a, b, preferred_element_type=F32)


def _dot_nt(a, b):
    return lax.dot_general(a, b, (((1,), (1,)), ((), ())), preferred_element_type=F32)


def _dot_tn(a, b):
    return lax.dot_general(a, b, (((0,), (0,)), ((), ())), preferred_element_type=F32)


def _load_cast_weight(w_hbm, w_vmem, stage_ref, sem_ref, rows):
    n_slots = stage_ref.shape[0] // rows
    n_chunks = w_hbm.shape[0] // rows
    assert sem_ref.shape[0] >= n_slots and n_chunks >= n_slots

    def slot_rows(slot):
        return pl.ds(pl.multiple_of(slot * rows, rows), rows)

    def chunk_copy(c, slot):
        return pltpu.make_async_copy(w_hbm.at[pl.ds(c * rows, rows), :],
                                     stage_ref.at[slot_rows(slot), :], sem_ref.at[slot])

    for c in range(n_slots - 1):
        chunk_copy(c, c).start()

    def body(c, carry):
        slot = c % n_slots
        chunk_copy(c, slot).wait()
        nxt = c + n_slots - 1

        @pl.when(nxt < n_chunks)
        def _prefetch():
            chunk_copy(nxt, nxt % n_slots).start()

        r0 = pl.multiple_of(c * rows, rows)
        w_vmem[pl.ds(r0, rows), :] = stage_ref[slot_rows(slot), :].astype(BF16)
        return carry

    lax.fori_loop(0, n_chunks, body, 0)


def _mix_kernel(chunk_decay,
                x_ref, meta_ref, lng_ref, win_hbm, cosb_ref, sinb_ref, cosm_ref, sinm_ref,
                coso_ref, sino_ref, mask_ref, qdec_ref, kdec_ref, gng_ref,
                dww_ref, dwb_ref, clng_ref, clnb_ref, pww_hbm, pwb_ref, wout_rows_ref,
                y_ref, wout_bf_ref,
                win_ref, pww_ref, sem_ref,
                state_ref, uext_ref, uph_ref, proj_ref, conv_ref, hn_ref, act_ref):
    i = pl.program_id(0)

    def meta_state(h, k_rot, v):
        kdec = kdec_ref[h, RCHUNK - CHUNK:RCHUNK, :]
        kd = (k_rot * jnp.concatenate([kdec, kdec], axis=-1)).astype(BF16)
        state_ref[h] = _dot_tn(kd, v.astype(BF16))

    @pl.when(i == 0)
    def _meta():
        stage_w = proj_ref.shape[1]
        for c0 in range(0, IN_WIDTH, stage_w):
            cols = pl.ds(c0, min(stage_w, IN_WIDTH - c0))
            _load_cast_weight(win_hbm.at[:, cols], win_ref.at[:, cols],
                              proj_ref.at[:, pl.ds(0, cols.size)], sem_ref, STAGE_ROWS)
        _load_cast_weight(pww_hbm, pww_ref, proj_ref.at[:, pl.ds(0, CONV_WIDTH)], sem_ref,
                          STAGE_ROWS)
        conv_ref[...] = jnp.zeros_like(conv_ref)
        proj_ref[...] = jnp.zeros_like(proj_ref)
        hn_ref[0:CHUNK - N_META, :] = jnp.zeros((CHUNK - N_META, D_MODEL), BF16)
        hn_ref[CHUNK - N_META:CHUNK, :] = _rms_norm_rows(meta_ref[...], lng_ref[...]).astype(BF16)
        hm = hn_ref[0:CHUNK, :]
        cm, sm = cosm_ref[0][0:1, :], sinm_ref[0][0:1, :]
        cos = cm * coso_ref[:CHUNK] - sm * sino_ref[:CHUNK]
        sin = sm * coso_ref[:CHUNK] + cm * sino_ref[:CHUNK]
        k = _dot(hm, win_ref[:, K0:K0 + RET_WIDTH])
        v = _dot(hm, win_ref[:, V0:V0 + RET_WIDTH])
        for h in range(RET_HEADS):
            hs = slice(h * HEAD_DIM, (h + 1) * HEAD_DIM)
            meta_state(h, _rotary(k[:, hs], cos, sin), v[:, hs])
        a = _dot(hm, win_ref[:, A0:A0 + CONV_WIDTH])
        b = _dot(hm, win_ref[:, B0:B0 + CONV_WIDTH])
        u = a * jax.nn.sigmoid(b)
        uext_ref[0:HIST, :] = u[CHUNK - HIST:, :]

    wout_bf_ref[...] = wout_rows_ref[...].astype(BF16)

    hn_ref[...] = _rms_norm_rows(x_ref[...], lng_ref[...]).astype(BF16)
    hn = hn_ref[...]

    def glu_projection():
        for c0 in range(0, CONV_WIDTH, GLU_BLOCK):
            a = _dot(hn, win_ref[:, A0 + c0:A0 + c0 + GLU_BLOCK])
            yield
            b = _dot(hn, win_ref[:, B0 + c0:B0 + c0 + GLU_BLOCK])
            uext_ref[HIST:HIST + TM, c0:c0 + GLU_BLOCK] = a * jax.nn.sigmoid(b)
            yield

    def layer_norm_slab(s, zero):
        rows = slice(s * SLAB, (s + 1) * SLAB)
        uf = _gate_rows(conv_ref[rows, :], zero)
        mu = jnp.mean(uf, axis=-1, keepdims=True)
        var = jnp.mean(jnp.square(uf - mu), axis=-1, keepdims=True)
        uf = (uf - mu) * lax.rsqrt(var + EPS) * clng_ref[...] + clnb_ref[...]
        sw = jax.nn.silu(uf)
        act_ref[rows, :] = sw.astype(BF16)
        return sw[:SUBLANES, :LANES]

    def gate_slab(s):
        rows = slice(s * SLAB, (s + 1) * SLAB)
        gc = proj_ref[rows, PROJ_GC:PROJ_GC + CONV_WIDTH]
        out = (conv_ref[rows, :] + pwb_ref[...]) * jax.nn.silu(gc)
        y_ref[rows, RET_WIDTH:] = out.astype(y_ref.dtype)

    def conv_tail():
        yield from _chained_slabs(TM // SLAB, LN_CHAINS, layer_norm_slab)
        conv_ref[...] = _dot(act_ref[...], pww_ref[...])
        yield from _slabs(TM // SLAB, gate_slab)

    retention = _retention_tail(i, chunk_decay, proj_ref, cosb_ref, sinb_ref, coso_ref, sino_ref,
                                mask_ref, qdec_ref, kdec_ref, gng_ref, state_ref, y_ref)
    _interleave([(glu_projection(), 1), (conv_tail(), TAIL_SLABS_PER_GLU_DOT),
                 (retention, RETENTION_SLABS_PER_GLU_DOT)])

    @pl.when(i < N_TILES)
    def _tile():
        _tile_body(hn_ref, win_ref, dww_ref, dwb_ref, uext_ref, uph_ref, proj_ref, conv_ref)


def _tile_body(hn_ref, win_ref, dww_ref, dwb_ref, uext_ref, uph_ref, proj_ref, conv_ref):
    proj_ref[:, 0:PROJ_GC] = _dot(hn_ref[...], win_ref[:, Q0:Q0 + PROJ_GC])
    proj_ref[:, PROJ_GC:] = _dot(hn_ref[...], win_ref[:, C0:C0 + CONV_WIDTH])

    n_phase = SUBLANES
    shift = HIST - (CONV_K - 1)
    n_acc = TM // SUBLANES
    window = uext_ref[...]
    for p in range(1, n_phase):
        uph_ref[p - 1, SUBLANES - p:SUBLANES - p + HIST + TM, :] = window

    def window_vreg(p, idx, cs):
        if p == 0:
            return uext_ref[SUBLANES * idx:SUBLANES * (idx + 1), cs]
        return uph_ref[p - 1, SUBLANES * (idx + 1):SUBLANES * (idx + 2), cs]

    tails = [None] * CONV_CHAINS
    for cbi in range(CONV_WIDTH // LANES):
        cs = slice(cbi * LANES, (cbi + 1) * LANES)
        bias = jnp.broadcast_to(dwb_ref[:, cs], (SUBLANES, LANES))
        wts = [jnp.broadcast_to(dww_ref[j:j + 1, cs], (SUBLANES, LANES)) for j in range(CONV_K)]
        out = []
        for m0 in range(0, n_acc, CONV_CHAINS):
            width = min(CONV_CHAINS, n_acc - m0)
            accs = [bias if tails[c] is None else bias + _ordered_zero(tails[c])
                    for c in range(width)]
            loaded = {}
            for j in range(CONV_K):
                p, a = (j + shift) % n_phase, (j + shift) // n_phase
                for c in range(width):
                    key = (p, a + m0 + c)
                    if key not in loaded:
                        loaded[key] = window_vreg(p, a + m0 + c, cs)
                    accs[c] = accs[c] + loaded[key] * wts[j]
            tails[:width] = accs
            out.extend(accs)
        conv_ref[:, cs] = jnp.concatenate(out, axis=0)
    uext_ref[0:HIST, :] = uext_ref[TM:TM + HIST, :]


def _retention_tail(i, chunk_decay, proj_ref, cosp_ref, sinp_ref, coso_ref, sino_ref, mask_ref,
                    qdec_ref, kdec_ref, gng_ref, state_ref, y_ref):
    n_slabs = TM // SLAB
    per_chunk = RCHUNK // SLAB
    cb, sb = cosp_ref[0][0:1, :], sinp_ref[0][0:1, :]
    cos = cb * coso_ref[...] - sb * sino_ref[...]
    sin = sb * coso_ref[...] + cb * sino_ref[...]
    twice = lambda t: jnp.concatenate([t, t], axis=-1)

    q_bf, k_bf, qd_bf, kd_bf, v_bf = ([[None] * n_slabs for _ in range(RET_HEADS)] for _ in range(5))

    def rotary_slab(u):
        h, s = divmod(u, n_slabs)
        rows = slice(s * SLAB, (s + 1) * SLAB)
        crow = slice((s % per_chunk) * SLAB, (s % per_chunk + 1) * SLAB)
        q_rot = _rotary(proj_ref[rows, Q0 + h * HEAD_DIM:Q0 + (h + 1) * HEAD_DIM], cos[rows], sin[rows])
        k_rot = _rotary(proj_ref[rows, K0 + h * HEAD_DIM:K0 + (h + 1) * HEAD_DIM], cos[rows], sin[rows])
        q_bf[h][s], k_bf[h][s] = q_rot.astype(BF16), k_rot.astype(BF16)
        qd_bf[h][s] = (q_rot * twice(qdec_ref[h, crow, :])).astype(BF16)
        kd_bf[h][s] = (k_rot * twice(kdec_ref[h, crow, :])).astype(BF16)
        v_bf[h][s] = proj_ref[rows, V0 + h * HEAD_DIM:V0 + (h + 1) * HEAD_DIM].astype(BF16)

    yield from _slabs(RET_HEADS * n_slabs, rotary_slab)

    for h in range(RET_HEADS):
        hs = slice(h * HEAD_DIM, (h + 1) * HEAD_DIM)
        decay = jnp.where(i == 0, 1.0, chunk_decay[h])
        for c in range(TM // RCHUNK):
            chunk = lambda pieces: jnp.concatenate(pieces[h][c * per_chunk:(c + 1) * per_chunk], axis=0)
            qc, kc, qdc, kdc, vc = (chunk(p) for p in (q_bf, k_bf, qd_bf, kd_bf, v_bf))
            scores = _dot_nt(qc, kc) * mask_ref[h]
            inner = _dot(scores.astype(BF16), vc)
            state = state_ref[h]
            cross = _dot(qdc, state.astype(BF16))
            update = _dot_tn(kdc, vc)

            def state_slab(s):
                rows = slice(s * STATE_SLAB, (s + 1) * STATE_SLAB)
                state_ref[h, rows, :] = state[rows] * decay + update[rows]

            yield from _slabs(HEAD_DIM // STATE_SLAB, state_slab)
            y = inner + cross

            def norm_slab(s):
                rows = slice(s * SLAB, (s + 1) * SLAB)
                trow = slice(c * RCHUNK + s * SLAB, c * RCHUNK + (s + 1) * SLAB)
                ys = y[rows]
                mu = jnp.mean(ys, axis=-1, keepdims=True)
                var = jnp.mean(jnp.square(ys - mu), axis=-1, keepdims=True)
                yn = (ys - mu) * lax.rsqrt(var + EPS) * gng_ref[:, hs]
                gate = proj_ref[trow, G0 + h * HEAD_DIM:G0 + (h + 1) * HEAD_DIM]
                y_ref[trow, hs] = (yn * jax.nn.silu(gate)).astype(y_ref.dtype)

            yield from _slabs(per_chunk, norm_slab)


def _out_kernel(y_ref, x_ref, wout_ref, fg_ref, o_ref):
    h = x_ref[...] + _dot(y_ref[...], wout_ref[...])
    o_ref[...] = _rms_norm_rows(h, fg_ref[...])


def _resident(shape):
    nd = len(shape)
    return pl.BlockSpec(shape, lambda i: (0,) * nd, pipeline_mode=pl.Buffered(1))


def kernel(x, meta_tokens, ln_g, w_in, ret_gn_g, conv_dw_w, conv_dw_b, conv_ln_g, conv_ln_b,
           conv_pw_w, conv_pw_b, w_out, final_g):
    assert x.shape == (1, SEQ, D_MODEL) and w_in.shape == (1, D_MODEL, IN_WIDTH)
    tabs, chunk_decay = _constant_tables()
    x2 = x[0]
    row = lambda a: a.reshape(1, -1)
    hbm = pl.BlockSpec(memory_space=pl.ANY)

    tile = lambda i: jnp.minimum(i, N_TILES - 1)
    prev_tile = lambda i: jnp.maximum(i - 1, 0)
    mix_in = [
        (x2, pl.BlockSpec((TM, D_MODEL), lambda i: (tile(i), 0))),
        (meta_tokens, _resident((N_META, D_MODEL))),
        (row(ln_g[0]), _resident((1, D_MODEL))),
        (w_in[0], hbm),
        (tabs["cos_b"], pl.BlockSpec((1, 8, HALF), lambda i: (prev_tile(i), 0, 0))),
        (tabs["sin_b"], pl.BlockSpec((1, 8, HALF), lambda i: (prev_tile(i), 0, 0))),
        (tabs["cos_b"], pl.BlockSpec((1, 8, HALF), lambda i: (N_TILES, 0, 0))),
        (tabs["sin_b"], pl.BlockSpec((1, 8, HALF), lambda i: (N_TILES, 0, 0))),
        (tabs["cos_o"], _resident((TM, HALF))),
        (tabs["sin_o"], _resident((TM, HALF))),
        (tabs["mask"], _resident((RET_HEADS, RCHUNK, RCHUNK))),
        (tabs["qdec"], _resident((RET_HEADS, RCHUNK, LANES))),
        (tabs["kdec"], _resident((RET_HEADS, RCHUNK, LANES))),
        (row(ret_gn_g[0]), _resident((1, RET_WIDTH))),
        (conv_dw_w[0], _resident((CONV_K, CONV_WIDTH))),
        (row(conv_dw_b[0]), _resident((1, CONV_WIDTH))),
        (row(conv_ln_g[0]), _resident((1, CONV_WIDTH))),
        (row(conv_ln_b[0]), _resident((1, CONV_WIDTH))),
        (conv_pw_w[0], hbm),
        (row(conv_pw_b[0]), _resident((1, CONV_WIDTH))),
        (w_out[0], pl.BlockSpec((WOUT_ROWS, D_MODEL), lambda i: (tile(i), 0))),
    ]
    y, w_out_bf = pl.pallas_call(
        functools.partial(_mix_kernel, chunk_decay),
        grid=(N_TILES + 1,),
        in_specs=[s for _, s in mix_in],
        out_specs=[pl.BlockSpec((TM, D_MODEL), lambda i: (prev_tile(i), 0)),
                   pl.BlockSpec((WOUT_ROWS, D_MODEL), lambda i: (tile(i), 0))],
        out_shape=[jax.ShapeDtypeStruct((SEQ, D_MODEL), BF16),
                   jax.ShapeDtypeStruct((D_MODEL, D_MODEL), BF16)],
        scratch_shapes=[
            pltpu.VMEM((D_MODEL, IN_WIDTH), BF16),
            pltpu.VMEM((CONV_WIDTH, CONV_WIDTH), BF16),
            pltpu.SemaphoreType.DMA((TM // STAGE_ROWS,)),
            pltpu.VMEM((RET_HEADS, HEAD_DIM, HEAD_DIM), F32),
            pltpu.VMEM((HIST + TM, CONV_WIDTH), F32),
            pltpu.VMEM((SUBLANES - 1, SUBLANES + HIST + TM, CONV_WIDTH), F32),
            pltpu.VMEM((TM, PROJ_GC + CONV_WIDTH), F32),
            pltpu.VMEM((TM, CONV_WIDTH), F32),
            pltpu.VMEM((TM, D_MODEL), BF16),
            pltpu.VMEM((TM, CONV_WIDTH), BF16),
        ],
        compiler_params=pltpu.CompilerParams(
            dimension_semantics=("arbitrary",), vmem_limit_bytes=VMEM_LIMIT_MIX),
        name="mix",
    )(*[a for a, _ in mix_in])

    out = pl.pallas_call(
        _out_kernel,
        grid=(SEQ // TM_OUT,),
        in_specs=[
            pl.BlockSpec((TM_OUT, D_MODEL), lambda i: (i, 0)),
            pl.BlockSpec((TM_OUT, D_MODEL), lambda i: (i, 0)),
            _resident((D_MODEL, D_MODEL)),
            _resident((1, D_MODEL)),
        ],
        out_specs=pl.BlockSpec((TM_OUT, D_MODEL), lambda i: (i, 0)),
        out_shape=jax.ShapeDtypeStruct((SEQ, D_MODEL), x.dtype),
        compiler_params=pltpu.CompilerParams(
            dimension_semantics=("arbitrary",), vmem_limit_bytes=VMEM_LIMIT_OUT),
        name="out",
    )(y, x2, w_out_bf, row(final_g))
    return out[None]
```

```python
import functools

import numpy as np
import jax
import jax.numpy as jnp
from jax import lax
from jax.experimental import pallas as pl
from jax.experimental.pallas import tpu as pltpu

D_MODEL = 2048
SEQ = 8192
N_META = 16
RET_WIDTH = 1024
RET_HEADS = 4
HEAD_DIM = 256
HALF = HEAD_DIM // 2
CONV_WIDTH = 1024
CONV_K = 31
CHUNK = 128
ROPE_BASE = 10000.0
EPS = 1e-6
IN_WIDTH = 4 * RET_WIDTH + 3 * CONV_WIDTH

Q0, K0, V0, G0, A0, B0, C0 = 0, 1024, 2048, 3072, 4096, 5120, 6144
PROJ_GC = 4 * RET_WIDTH

LANES = 128
SUBLANES = 8
TM = 256
N_TILES = SEQ // TM
RCHUNK = TM
HIST = 32
TM_OUT = 512
CONV_CHAINS = 2
SLAB = 16
STATE_SLAB = 64
LN_CHAINS = 4
GLU_BLOCK = 256
TAIL_SLABS_PER_GLU_DOT = 3
RETENTION_SLABS_PER_GLU_DOT = 32
STAGE_ROWS = 16
WOUT_ROWS = D_MODEL // N_TILES
VMEM_CAPACITY = 64 * 1024 * 1024
VMEM_LIMIT_MIX = VMEM_CAPACITY - 1024 * 1024
VMEM_LIMIT_OUT = 48 * 1024 * 1024

BF16 = jnp.bfloat16
F32 = jnp.float32


def _constant_tables():
    half = HALF
    inv_freq = ROPE_BASE ** (-np.arange(half, dtype=np.float64) / half)
    bases = np.concatenate([N_META + TM * np.arange(N_TILES, dtype=np.float64),
                            [-(CHUNK - N_META)]])
    ang_b = bases[:, None] * inv_freq[None, :]
    ang_o = np.arange(TM, dtype=np.float64)[:, None] * inv_freq[None, :]
    cos_b = np.broadcast_to(np.cos(ang_b)[:, None, :], (N_TILES + 1, 8, half))
    sin_b = np.broadcast_to(np.sin(ang_b)[:, None, :], (N_TILES + 1, 8, half))
    gamma = 1.0 - np.exp2(-5.0 - np.arange(RET_HEADS, dtype=np.float64))
    log_g = np.log(gamma)
    idx = np.arange(RCHUNK, dtype=np.float64)
    rel = idx[:, None] - idx[None, :]
    scale = HEAD_DIM ** -0.5
    mask = np.where(rel[None] >= 0, np.exp(np.maximum(rel, 0.0)[None] * log_g[:, None, None]), 0.0)
    q_decay = np.exp((idx[None, :] + 1.0) * log_g[:, None])
    k_decay = np.exp((RCHUNK - 1.0 - idx[None, :]) * log_g[:, None])
    chunk_decay = np.exp(RCHUNK * log_g)
    qdec = np.broadcast_to(q_decay[:, :, None], (RET_HEADS, RCHUNK, LANES))
    kdec = np.broadcast_to((k_decay * scale)[:, :, None], (RET_HEADS, RCHUNK, LANES))
    f = lambda a: jnp.asarray(np.ascontiguousarray(a), dtype=F32)
    tabs = dict(cos_b=f(cos_b), sin_b=f(sin_b), cos_o=f(np.cos(ang_o)), sin_o=f(np.sin(ang_o)),
                mask=f(mask * scale), qdec=f(qdec), kdec=f(kdec))
    return tabs, tuple(float(c) for c in chunk_decay)


def _rms_norm_rows(xf, g):
    ms = jnp.mean(xf * xf, axis=-1, keepdims=True)
    return xf * lax.rsqrt(ms + EPS) * g


def _rotary(x, cos, sin):
    x1, x2 = x[:, :HALF], x[:, HALF:]
    return jnp.concatenate([x1 * cos - x2 * sin, x1 * sin + x2 * cos], axis=-1)


def _ordered_zero(v):
    bits = lax.bitcast_convert_type(v, jnp.uint32)
    return lax.bitcast_convert_type((bits >> 16) >> 16, F32)


def _gate_rows(x, zero):
    if zero is None:
        return x
    head = x[:, :LANES] + jnp.concatenate([zero] * (x.shape[0] // SUBLANES), axis=0)
    return jnp.concatenate([head, x[:, LANES:]], axis=1)


def _chained_slabs(n_slabs, n_chains, slab_fn):
    tails = [None] * n_chains
    for s in range(n_slabs):
        c = s % n_chains
        tails[c] = slab_fn(s, None if tails[c] is None else _ordered_zero(tails[c]))
        yield


def _slabs(n_slabs, slab_fn):
    for s in range(n_slabs):
        slab_fn(s)
        yield


def _interleave(streams):
    live = list(streams)
    while live:
        for entry in list(live):
            stream, weight = entry
            for _ in range(weight):
                if next(stream, _interleave) is _interleave:
                    live.remove(entry)
                    break


def _dot(a, b):
    return jnp.dot(a, b, preferred_element_type=F32)


def _dot_nt(a, b):
    return lax.dot_general(a, b, (((1,), (1,)), ((), ())), preferred_element_type=F32)


def _dot_tn(a, b):
    return lax.dot_general(a, b, (((0,), (0,)), ((), ())), preferred_element_type=F32)


def _load_cast_weight(w_hbm, w_vmem, stage_ref, sem_ref, rows):
    n_slots = stage_ref.shape[0] // rows
    n_chunks = w_hbm.shape[0] // rows
    assert sem_ref.shape[0] >= n_slots and n_chunks >= n_slots

    def slot_rows(slot):
        return pl.ds(pl.multiple_of(slot * rows, rows), rows)

    def chunk_copy(c, slot):
        return pltpu.make_async_copy(w_hbm.at[pl.ds(c * rows, rows), :],
                                     stage_ref.at[slot_rows(slot), :], sem_ref.at[slot])

    for c in range(n_slots - 1):
        chunk_copy(c, c).start()

    def body(c, carry):
        slot = c % n_slots
        chunk_copy(c, slot).wait()
        nxt = c + n_slots - 1

        @pl.when(nxt < n_chunks)
        def _prefetch():
            chunk_copy(nxt, nxt % n_slots).start()

        r0 = pl.multiple_of(c * rows, rows)
        w_vmem[pl.ds(r0, rows), :] = stage_ref[slot_rows(slot), :].astype(BF16)
        return carry

    lax.fori_loop(0, n_chunks, body, 0)


def _mix_kernel(chunk_decay,
                x_ref, meta_ref, lng_ref, win_hbm, cosb_ref, sinb_ref, cosm_ref, sinm_ref,
                coso_ref, sino_ref, mask_ref, qdec_ref, kdec_ref, gng_ref,
                dww_ref, dwb_ref, clng_ref, clnb_ref, pww_hbm, pwb_ref, wout_rows_ref,
                y_ref, wout_bf_ref,
                win_ref, pww_ref, sem_ref,
                state_ref, uext_ref, uph_ref, proj_ref, conv_ref, hn_ref, act_ref):
    i = pl.program_id(0)

    def meta_state(h, k_rot, v):
        kdec = kdec_ref[h, RCHUNK - CHUNK:RCHUNK, :]
        kd = (k_rot * jnp.concatenate([kdec, kdec], axis=-1)).astype(BF16)
        state_ref[h] = _dot_tn(kd, v.astype(BF16))

    @pl.when(i == 0)
    def _meta():
        stage_w = proj_ref.shape[1]
        for c0 in range(0, IN_WIDTH, stage_w):
            cols = pl.ds(c0, min(stage_w, IN_WIDTH - c0))
            _load_cast_weight(win_hbm.at[:, cols], win_ref.at[:, cols],
                              proj_ref.at[:, pl.ds(0, cols.size)], sem_ref, STAGE_ROWS)
        _load_cast_weight(pww_hbm, pww_ref, proj_ref.at[:, pl.ds(0, CONV_WIDTH)], sem_ref,
                          STAGE_ROWS)
        conv_ref[...] = jnp.zeros_like(conv_ref)
        proj_ref[...] = jnp.zeros_like(proj_ref)
        hn_ref[0:CHUNK - N_META, :] = jnp.zeros((CHUNK - N_META, D_MODEL), BF16)
        hn_ref[CHUNK - N_META:CHUNK, :] = _rms_norm_rows(meta_ref[...], lng_ref[...]).astype(BF16)
        hm = hn_ref[0:CHUNK, :]
        cm, sm = cosm_ref[0][0:1, :], sinm_ref[0][0:1, :]
        cos = cm * coso_ref[:CHUNK] - sm * sino_ref[:CHUNK]
        sin = sm * coso_ref[:CHUNK] + cm * sino_ref[:CHUNK]
        k = _dot(hm, win_ref[:, K0:K0 + RET_WIDTH])
        v = _dot(hm, win_ref[:, V0:V0 + RET_WIDTH])
        for h in range(RET_HEADS):
            hs = slice(h * HEAD_DIM, (h + 1) * HEAD_DIM)
            meta_state(h, _rotary(k[:, hs], cos, sin), v[:, hs])
        a = _dot(hm, win_ref[:, A0:A0 + CONV_WIDTH])
        b = _dot(hm, win_ref[:, B0:B0 + CONV_WIDTH])
        u = a * jax.nn.sigmoid(b)
        uext_ref[0:HIST, :] = u[CHUNK - HIST:, :]

    hn_ref[...] = _rms_norm_rows(x_ref[...], lng_ref[...]).astype(BF16)
    hn = hn_ref[...]

    def glu_projection():
        for c0 in range(0, CONV_WIDTH, GLU_BLOCK):
            a = _dot(hn, win_ref[:, A0 + c0:A0 + c0 + GLU_BLOCK])
            yield
            b = _dot(hn, win_ref[:, B0 + c0:B0 + c0 + GLU_BLOCK])
            uext_ref[HIST:HIST + TM, c0:c0 + GLU_BLOCK] = a * jax.nn.sigmoid(b)
            yield

    def layer_norm_slab(s, zero):
        rows = slice(s * SLAB, (s + 1) * SLAB)
        uf = _gate_rows(conv_ref[rows, :], zero)
        mu = jnp.mean(uf, axis=-1, keepdims=True)
        var = jnp.mean(jnp.square(uf - mu), axis=-1, keepdims=True)
        uf = (uf - mu) * lax.rsqrt(var + EPS) * clng_ref[...] + clnb_ref[...]
        sw = jax.nn.silu(uf)
        act_ref[rows, :] = sw.astype(BF16)
        return sw[:SUBLANES, :LANES]

    def gate_slab(s):
        rows = slice(s * SLAB, (s + 1) * SLAB)
        gc = proj_ref[rows, PROJ_GC:PROJ_GC + CONV_WIDTH]
        out = (conv_ref[rows, :] + pwb_ref[...]) * jax.nn.silu(gc)
        y_ref[rows, RET_WIDTH:] = out.astype(y_ref.dtype)

    def conv_tail():
        yield from _chained_slabs(TM // SLAB, LN_CHAINS, layer_norm_slab)
        conv_ref[...] = _dot(act_ref[...], pww_ref[...])
        yield from _slabs(TM // SLAB, gate_slab)

    retention = _retention_tail(i, chunk_decay, proj_ref, cosb_ref, sinb_ref, coso_ref, sino_ref,
                                mask_ref, qdec_ref, kdec_ref, gng_ref, state_ref, y_ref)
    _interleave([(glu_projection(), 1), (conv_tail(), TAIL_SLABS_PER_GLU_DOT),
                 (retention, RETENTION_SLABS_PER_GLU_DOT)])

    wout_bf_ref[...] = wout_rows_ref[...].astype(BF16)

    @pl.when(i < N_TILES)
    def _tile():
        _tile_body(hn_ref, win_ref, dww_ref, dwb_ref, uext_ref, uph_ref, proj_ref, conv_ref)


def _tile_body(hn_ref, win_ref, dww_ref, dwb_ref, uext_ref, uph_ref, proj_ref, conv_ref):
    proj_ref[:, 0:PROJ_GC] = _dot(hn_ref[...], win_ref[:, Q0:Q0 + PROJ_GC])
    proj_ref[:, PROJ_GC:] = _dot(hn_ref[...], win_ref[:, C0:C0 + CONV_WIDTH])

    n_phase = SUBLANES
    shift = HIST - (CONV_K - 1)
    n_acc = TM // SUBLANES
    window = uext_ref[...]
    for p in range(1, n_phase):
        uph_ref[p - 1, SUBLANES - p:SUBLANES - p + HIST + TM, :] = window

    def window_vreg(p, idx, cs):
        if p == 0:
            return uext_ref[SUBLANES * idx:SUBLANES * (idx + 1), cs]
        return uph_ref[p - 1, SUBLANES * (idx + 1):SUBLANES * (idx + 2), cs]

    tails = [None] * CONV_CHAINS
    for cbi in range(CONV_WIDTH // LANES):
        cs = slice(cbi * LANES, (cbi + 1) * LANES)
        bias = jnp.broadcast_to(dwb_ref[:, cs], (SUBLANES, LANES))
        wts = [jnp.broadcast_to(dww_ref[j:j + 1, cs], (SUBLANES, LANES)) for j in range(CONV_K)]
        out = []
        for m0 in range(0, n_acc, CONV_CHAINS):
            width = min(CONV_CHAINS, n_acc - m0)
            accs = [bias if tails[c] is None else bias + _ordered_zero(tails[c])
                    for c in range(width)]
            loaded = {}
            for j in range(CONV_K):
                p, a = (j + shift) % n_phase, (j + shift) // n_phase
                for c in range(width):
                    key = (p, a + m0 + c)
                    if key not in loaded:
                        loaded[key] = window_vreg(p, a + m0 + c, cs)
                    accs[c] = accs[c] + loaded[key] * wts[j]
            tails[:width] = accs
            out.extend(accs)
        conv_ref[:, cs] = jnp.concatenate(out, axis=0)
    uext_ref[0:HIST, :] = uext_ref[TM:TM + HIST, :]


def _retention_tail(i, chunk_decay, proj_ref, cosp_ref, sinp_ref, coso_ref, sino_ref, mask_ref,
                    qdec_ref, kdec_ref, gng_ref, state_ref, y_ref):
    n_slabs = TM // SLAB
    per_chunk = RCHUNK // SLAB
    cb, sb = cosp_ref[0][0:1, :], sinp_ref[0][0:1, :]
    cos = cb * coso_ref[...] - sb * sino_ref[...]
    sin = sb * coso_ref[...] + cb * sino_ref[...]
    twice = lambda t: jnp.concatenate([t, t], axis=-1)

    q_bf, k_bf, qd_bf, kd_bf, v_bf = ([[None] * n_slabs for _ in range(RET_HEADS)] for _ in range(5))

    def rotary_slab(u):
        h, s = divmod(u, n_slabs)
        rows = slice(s * SLAB, (s + 1) * SLAB)
        crow = slice((s % per_chunk) * SLAB, (s % per_chunk + 1) * SLAB)
        q_rot = _rotary(proj_ref[rows, Q0 + h * HEAD_DIM:Q0 + (h + 1) * HEAD_DIM], cos[rows], sin[rows])
        k_rot = _rotary(proj_ref[rows, K0 + h * HEAD_DIM:K0 + (h + 1) * HEAD_DIM], cos[rows], sin[rows])
        q_bf[h][s], k_bf[h][s] = q_rot.astype(BF16), k_rot.astype(BF16)
        qd_bf[h][s] = (q_rot * twice(qdec_ref[h, crow, :])).astype(BF16)
        kd_bf[h][s] = (k_rot * twice(kdec_ref[h, crow, :])).astype(BF16)
        v_bf[h][s] = proj_ref[rows, V0 + h * HEAD_DIM:V0 + (h + 1) * HEAD_DIM].astype(BF16)

    yield from _slabs(RET_HEADS * n_slabs, rotary_slab)

    for h in range(RET_HEADS):
        hs = slice(h * HEAD_DIM, (h + 1) * HEAD_DIM)
        decay = jnp.where(i == 0, 1.0, chunk_decay[h])
        for c in range(TM // RCHUNK):
            chunk = lambda pieces: jnp.concatenate(pieces[h][c * per_chunk:(c + 1) * per_chunk], axis=0)
            qc, kc, qdc, kdc, vc = (chunk(p) for p in (q_bf, k_bf, qd_bf, kd_bf, v_bf))
            scores = _dot_nt(qc, kc) * mask_ref[h]
            inner = _dot(scores.astype(BF16), vc)
            state = state_ref[h]
            cross = _dot(qdc, state.astype(BF16))
            update = _dot_tn(kdc, vc)

            def state_slab(s):
                rows = slice(s * STATE_SLAB, (s + 1) * STATE_SLAB)
                state_ref[h, rows, :] = state[rows] * decay + update[rows]

            yield from _slabs(HEAD_DIM // STATE_SLAB, state_slab)
            y = inner + cross

            def norm_slab(s):
                rows = slice(s * SLAB, (s + 1) * SLAB)
                trow = slice(c * RCHUNK + s * SLAB, c * RCHUNK + (s + 1) * SLAB)
                ys = y[rows]
                mu = jnp.mean(ys, axis=-1, keepdims=True)
                var = jnp.mean(jnp.square(ys - mu), axis=-1, keepdims=True)
                yn = (ys - mu) * lax.rsqrt(var + EPS) * gng_ref[:, hs]
                gate = proj_ref[trow, G0 + h * HEAD_DIM:G0 + (h + 1) * HEAD_DIM]
                y_ref[trow, hs] = (yn * jax.nn.silu(gate)).astype(y_ref.dtype)

            yield from _slabs(per_chunk, norm_slab)


def _out_kernel(y_ref, x_ref, wout_ref, fg_ref, o_ref):
    h = x_ref[...] + _dot(y_ref[...], wout_ref[...])
    o_ref[...] = _rms_norm_rows(h, fg_ref[...])


def _resident(shape):
    nd = len(shape)
    return pl.BlockSpec(shape, lambda i: (0,) * nd, pipeline_mode=pl.Buffered(1))


def kernel(x, meta_tokens, ln_g, w_in, ret_gn_g, conv_dw_w, conv_dw_b, conv_ln_g, conv_ln_b,
           conv_pw_w, conv_pw_b, w_out, final_g):
    assert x.shape == (1, SEQ, D_MODEL) and w_in.shape == (1, D_MODEL, IN_WIDTH)
    tabs, chunk_decay = _constant_tables()
    x2 = x[0]
    row = lambda a: a.reshape(1, -1)
    hbm = pl.BlockSpec(memory_space=pl.ANY)

    tile = lambda i: jnp.minimum(i, N_TILES - 1)
    prev_tile = lambda i: jnp.maximum(i - 1, 0)
    mix_in = [
        (x2, pl.BlockSpec((TM, D_MODEL), lambda i: (tile(i), 0))),
        (meta_tokens, _resident((N_META, D_MODEL))),
        (row(ln_g[0]), _resident((1, D_MODEL))),
        (w_in[0], hbm),
        (tabs["cos_b"], pl.BlockSpec((1, 8, HALF), lambda i: (prev_tile(i), 0, 0))),
        (tabs["sin_b"], pl.BlockSpec((1, 8, HALF), lambda i: (prev_tile(i), 0, 0))),
        (tabs["cos_b"], pl.BlockSpec((1, 8, HALF), lambda i: (N_TILES, 0, 0))),
        (tabs["sin_b"], pl.BlockSpec((1, 8, HALF), lambda i: (N_TILES, 0, 0))),
        (tabs["cos_o"], _resident((TM, HALF))),
        (tabs["sin_o"], _resident((TM, HALF))),
        (tabs["mask"], _resident((RET_HEADS, RCHUNK, RCHUNK))),
        (tabs["qdec"], _resident((RET_HEADS, RCHUNK, LANES))),
        (tabs["kdec"], _resident((RET_HEADS, RCHUNK, LANES))),
        (row(ret_gn_g[0]), _resident((1, RET_WIDTH))),
        (conv_dw_w[0], _resident((CONV_K, CONV_WIDTH))),
        (row(conv_dw_b[0]), _resident((1, CONV_WIDTH))),
        (row(conv_ln_g[0]), _resident((1, CONV_WIDTH))),
        (row(conv_ln_b[0]), _resident((1, CONV_WIDTH))),
        (conv_pw_w[0], hbm),
        (row(conv_pw_b[0]), _resident((1, CONV_WIDTH))),
        (w_out[0], pl.BlockSpec((WOUT_ROWS, D_MODEL), lambda i: (tile(i), 0))),
    ]
    y, w_out_bf = pl.pallas_call(
        functools.partial(_mix_kernel, chunk_decay),
        grid=(N_TILES + 1,),
        in_specs=[s for _, s in mix_in],
        out_specs=[pl.BlockSpec((TM, D_MODEL), lambda i: (prev_tile(i), 0)),
                   pl.BlockSpec((WOUT_ROWS, D_MODEL), lambda i: (tile(i), 0))],
        out_shape=[jax.ShapeDtypeStruct((SEQ, D_MODEL), BF16),
                   jax.ShapeDtypeStruct((D_MODEL, D_MODEL), BF16)],
        scratch_shapes=[
            pltpu.VMEM((D_MODEL, IN_WIDTH), BF16),
            pltpu.VMEM((CONV_WIDTH, CONV_WIDTH), BF16),
            pltpu.SemaphoreType.DMA((TM // STAGE_ROWS,)),
            pltpu.VMEM((RET_HEADS, HEAD_DIM, HEAD_DIM), F32),
            pltpu.VMEM((HIST + TM, CONV_WIDTH), F32),
            pltpu.VMEM((SUBLANES - 1, SUBLANES + HIST + TM, CONV_WIDTH), F32),
            pltpu.VMEM((TM, PROJ_GC + CONV_WIDTH), F32),
            pltpu.VMEM((TM, CONV_WIDTH), F32),
            pltpu.VMEM((TM, D_MODEL), BF16),
            pltpu.VMEM((TM, CONV_WIDTH), BF16),
        ],
        compiler_params=pltpu.CompilerParams(
            dimension_semantics=("arbitrary",), vmem_limit_bytes=VMEM_LIMIT_MIX),
        name="mix",
    )(*[a for a, _ in mix_in])

    out = pl.pallas_call(
        _out_kernel,
        grid=(SEQ // TM_OUT,),
        in_specs=[
            pl.BlockSpec((TM_OUT, D_MODEL), lambda i: (i, 0)),
            pl.BlockSpec((TM_OUT, D_MODEL), lambda i: (i, 0)),
            _resident((D_MODEL, D_MODEL)),
            _resident((1, D_MODEL)),
        ],
        out_specs=pl.BlockSpec((TM_OUT, D_MODEL), lambda i: (i, 0)),
        out_shape=jax.ShapeDtypeStruct((SEQ, D_MODEL), x.dtype),
        compiler_params=pltpu.CompilerParams(
            dimension_semantics=("arbitrary",), vmem_limit_bytes=VMEM_LIMIT_OUT),
        name="out",
    )(y, x2, w_out_bf, row(final_g))
    return out[None]
```

```python
import functools

import numpy as np
import jax
import jax.numpy as jnp
from jax import lax
from jax.experimental import pallas as pl
from jax.experimental.pallas import tpu as pltpu

D_MODEL = 2048
SEQ = 8192
N_META = 16
RET_WIDTH = 1024
RET_HEADS = 4
HEAD_DIM = 256
HALF = HEAD_DIM // 2
CONV_WIDTH = 1024
CONV_K = 31
CHUNK = 128
ROPE_BASE = 10000.0
EPS = 1e-6
IN_WIDTH = 4 * RET_WIDTH + 3 * CONV_WIDTH

Q0, K0, V0, G0, A0, B0, C0 = 0, 1024, 2048, 3072, 4096, 5120, 6144
PROJ_GC = 4 * RET_WIDTH

LANES = 128
SUBLANES = 8
TM = 256
N_TILES = SEQ // TM
RCHUNK = TM
HIST = 32
TM_OUT = 512
CONV_CHAINS = 2
SLAB = 16
STATE_SLAB = 64
LN_CHAINS = 4
GLU_BLOCK = 256
TAIL_SLABS_PER_GLU_DOT = 3
RETENTION_SLABS_PER_GLU_DOT = 32
STAGE_ROWS = 16
WOUT_ROWS = D_MODEL // N_TILES
VMEM_CAPACITY = 64 * 1024 * 1024
VMEM_LIMIT_MIX = VMEM_CAPACITY - 1024 * 1024
VMEM_LIMIT_OUT = 48 * 1024 * 1024

BF16 = jnp.bfloat16
F32 = jnp.float32


def _constant_tables():
    half = HALF
    inv_freq = ROPE_BASE ** (-np.arange(half, dtype=np.float64) / half)
    bases = np.concatenate([N_META + TM * np.arange(N_TILES, dtype=np.float64),
                            [-(CHUNK - N_META)]])
    ang_b = bases[:, None] * inv_freq[None, :]
    ang_o = np.arange(TM, dtype=np.float64)[:, None] * inv_freq[None, :]
    cos_b = np.broadcast_to(np.cos(ang_b)[:, None, :], (N_TILES + 1, 8, half))
    sin_b = np.broadcast_to(np.sin(ang_b)[:, None, :], (N_TILES + 1, 8, half))
    gamma = 1.0 - np.exp2(-5.0 - np.arange(RET_HEADS, dtype=np.float64))
    log_g = np.log(gamma)
    idx = np.arange(RCHUNK, dtype=np.float64)
    rel = idx[:, None] - idx[None, :]
    scale = HEAD_DIM ** -0.5
    mask = np.where(rel[None] >= 0, np.exp(np.maximum(rel, 0.0)[None] * log_g[:, None, None]), 0.0)
    q_decay = np.exp((idx[None, :] + 1.0) * log_g[:, None])
    k_decay = np.exp((RCHUNK - 1.0 - idx[None, :]) * log_g[:, None])
    chunk_decay = np.exp(RCHUNK * log_g)
    qdec = np.broadcast_to(q_decay[:, :, None], (RET_HEADS, RCHUNK, LANES))
    kdec = np.broadcast_to((k_decay * scale)[:, :, None], (RET_HEADS, RCHUNK, LANES))
    f = lambda a: jnp.asarray(np.ascontiguousarray(a), dtype=F32)
    tabs = dict(cos_b=f(cos_b), sin_b=f(sin_b), cos_o=f(np.cos(ang_o)), sin_o=f(np.sin(ang_o)),
                mask=f(mask * scale), qdec=f(qdec), kdec=f(kdec))
    return tabs, tuple(float(c) for c in chunk_decay)


def _rms_norm_rows(xf, g):
    ms = jnp.mean(xf * xf, axis=-1, keepdims=True)
    return xf * lax.rsqrt(ms + EPS) * g


def _rotary(x, cos, sin):
    x1, x2 = x[:, :HALF], x[:, HALF:]
    return jnp.concatenate([x1 * cos - x2 * sin, x1 * sin + x2 * cos], axis=-1)


def _ordered_zero(v):
    bits = lax.bitcast_convert_type(v, jnp.uint32)
    return lax.bitcast_convert_type((bits >> 16) >> 16, F32)


def _gate_rows(x, zero):
    if zero is None:
        return x
    head = x[:, :LANES] + jnp.concatenate([zero] * (x.shape[0] // SUBLANES), axis=0)
    return jnp.concatenate([head, x[:, LANES:]], axis=1)


def _chained_slabs(n_slabs, n_chains, slab_fn):
    tails = [None] * n_chains
    for s in range(n_slabs):
        c = s % n_chains
        tails[c] = slab_fn(s, None if tails[c] is None else _ordered_zero(tails[c]))
        yield


def _slabs(n_slabs, slab_fn):
    for s in range(n_slabs):
        slab_fn(s)
        yield


def _interleave(streams):
    live = list(streams)
    while live:
        for entry in list(live):
            stream, weight = entry
            for _ in range(weight):
                if next(stream, _interleave) is _interleave:
                    live.remove(entry)
                    break


def _dot(a, b):
    return jnp.dot(a, b, preferred_element_type=F32)


def _dot_nt(a, b):
    return lax.dot_general(a, b, (((1,), (1,)), ((), ())), preferred_element_type=F32)


def _dot_tn(a, b):
    return lax.dot_general(a, b, (((0,), (0,)), ((), ())), preferred_element_type=F32)


def _load_cast_weight(w_hbm, w_vmem, stage_ref, sem_ref, rows):
    n_slots = stage_ref.shape[0] // rows
    n_chunks = w_hbm.shape[0] // rows
    assert sem_ref.shape[0] >= n_slots and n_chunks >= n_slots

    def slot_rows(slot):
        return pl.ds(pl.multiple_of(slot * rows, rows), rows)

    def chunk_copy(c, slot):
        return pltpu.make_async_copy(w_hbm.at[pl.ds(c * rows, rows), :],
                                     stage_ref.at[slot_rows(slot), :], sem_ref.at[slot])

    for c in range(n_slots - 1):
        chunk_copy(c, c).start()

    def body(c, carry):
        slot = c % n_slots
        chunk_copy(c, slot).wait()
        nxt = c + n_slots - 1

        @pl.when(nxt < n_chunks)
        def _prefetch():
            chunk_copy(nxt, nxt % n_slots).start()

        r0 = pl.multiple_of(c * rows, rows)
        w_vmem[pl.ds(r0, rows), :] = stage_ref[slot_rows(slot), :].astype(BF16)
        return carry

    lax.fori_loop(0, n_chunks, body, 0)


def _mix_kernel(chunk_decay,
                x_ref, meta_ref, lng_ref, win_hbm, cosb_ref, sinb_ref, cosm_ref, sinm_ref,
                coso_ref, sino_ref, mask_ref, qdec_ref, kdec_ref, gng_ref,
                dww_ref, dwb_ref, clng_ref, clnb_ref, pww_hbm, pwb_ref, wout_rows_ref,
                y_ref, wout_bf_ref,
                win_ref, pww_ref, sem_ref,
                state_ref, uext_ref, uph_ref, proj_ref, conv_ref, hn_ref, act_ref):
    i = pl.program_id(0)

    def meta_state(h, k_rot, v):
        kdec = kdec_ref[h, RCHUNK - CHUNK:RCHUNK, :]
        kd = (k_rot * jnp.concatenate([kdec, kdec], axis=-1)).astype(BF16)
        state_ref[h] = _dot_tn(kd, v.astype(BF16))

    @pl.when(i == 0)
    def _meta():
        stage_w = proj_ref.shape[1]
        for c0 in range(0, IN_WIDTH, stage_w):
            cols = pl.ds(c0, min(stage_w, IN_WIDTH - c0))
            _load_cast_weight(win_hbm.at[:, cols], win_ref.at[:, cols],
                              proj_ref.at[:, pl.ds(0, cols.size)], sem_ref, STAGE_ROWS)
        _load_cast_weight(pww_hbm, pww_ref, proj_ref.at[:, pl.ds(0, CONV_WIDTH)], sem_ref,
                          STAGE_ROWS)
        conv_ref[...] = jnp.zeros_like(conv_ref)
        proj_ref[...] = jnp.zeros_like(proj_ref)
        hn_ref[0:CHUNK - N_META, :] = jnp.zeros((CHUNK - N_META, D_MODEL), BF16)
        hn_ref[CHUNK - N_META:CHUNK, :] = _rms_norm_rows(meta_ref[...], lng_ref[...]).astype(BF16)
        hm = hn_ref[0:CHUNK, :]
        cm, sm = cosm_ref[0][0:1, :], sinm_ref[0][0:1, :]
        cos = cm * coso_ref[:CHUNK] - sm * sino_ref[:CHUNK]
        sin = sm * coso_ref[:CHUNK] + cm * sino_ref[:CHUNK]
        k = _dot(hm, win_ref[:, K0:K0 + RET_WIDTH])
        v = _dot(hm, win_ref[:, V0:V0 + RET_WIDTH])
        for h in range(RET_HEADS):
            hs = slice(h * HEAD_DIM, (h + 1) * HEAD_DIM)
            meta_state(h, _rotary(k[:, hs], cos, sin), v[:, hs])
        a = _dot(hm, win_ref[:, A0:A0 + CONV_WIDTH])
        b = _dot(hm, win_ref[:, B0:B0 + CONV_WIDTH])
        u = a * jax.nn.sigmoid(b)
        uext_ref[0:HIST, :] = u[CHUNK - HIST:, :]

    hn_ref[...] = _rms_norm_rows(x_ref[...], lng_ref[...]).astype(BF16)
    hn = hn_ref[...]

    def glu_projection():
        for c0 in range(0, CONV_WIDTH, GLU_BLOCK):
            a = _dot(hn, win_ref[:, A0 + c0:A0 + c0 + GLU_BLOCK])
            yield
            b = _dot(hn, win_ref[:, B0 + c0:B0 + c0 + GLU_BLOCK])
            uext_ref[HIST:HIST + TM, c0:c0 + GLU_BLOCK] = a * jax.nn.sigmoid(b)
            yield

    def layer_norm_slab(s, zero):
        rows = slice(s * SLAB, (s + 1) * SLAB)
        uf = _gate_rows(conv_ref[rows, :], zero)
        mu = jnp.mean(uf, axis=-1, keepdims=True)
        var = jnp.mean(jnp.square(uf - mu), axis=-1, keepdims=True)
        uf = (uf - mu) * lax.rsqrt(var + EPS) * clng_ref[...] + clnb_ref[...]
        sw = jax.nn.silu(uf)
        act_ref[rows, :] = sw.astype(BF16)
        return sw[:SUBLANES, :LANES]

    def gate_slab(s):
        rows = slice(s * SLAB, (s + 1) * SLAB)
        gc = proj_ref[rows, PROJ_GC:PROJ_GC + CONV_WIDTH]
        out = (conv_ref[rows, :] + pwb_ref[...]) * jax.nn.silu(gc)
        y_ref[rows, RET_WIDTH:] = out.astype(y_ref.dtype)

    def conv_tail():
        yield from _chained_slabs(TM // SLAB, LN_CHAINS, layer_norm_slab)
        conv_ref[...] = _dot(act_ref[...], pww_ref[...])
        yield from _slabs(TM // SLAB, gate_slab)

    retention = _retention_tail(i, chunk_decay, proj_ref, cosb_ref, sinb_ref, coso_ref, sino_ref,
                                mask_ref, qdec_ref, kdec_ref, gng_ref, state_ref, y_ref)
    _interleave([(glu_projection(), 1), (conv_tail(), TAIL_SLABS_PER_GLU_DOT),
                 (retention, RETENTION_SLABS_PER_GLU_DOT)])

    wout_bf_ref[...] = wout_rows_ref[...].astype(BF16)

    @pl.when(i < N_TILES)
    def _tile():
        _tile_body(hn_ref, win_ref, dww_ref, dwb_ref, uext_ref, uph_ref, proj_ref, conv_ref)


def _tile_body(hn_ref, win_ref, dww_ref, dwb_ref, uext_ref, uph_ref, proj_ref, conv_ref):
    proj_ref[:, 0:PROJ_GC] = _dot(hn_ref[...], win_ref[:, Q0:Q0 + PROJ_GC])
    proj_ref[:, PROJ_GC:] = _dot(hn_ref[...], win_ref[:, C0:C0 + CONV_WIDTH])

    n_phase = SUBLANES
    shift = HIST - (CONV_K - 1)
    n_acc = TM // SUBLANES
    window = uext_ref[...]
    for p in range(1, n_phase):
        uph_ref[p - 1, SUBLANES - p:SUBLANES - p + HIST + TM, :] = window

    def window_vreg(p, idx, cs):
        if p == 0:
            return uext_ref[SUBLANES * idx:SUBLANES * (idx + 1), cs]
        return uph_ref[p - 1, SUBLANES * (idx + 1):SUBLANES * (idx + 2), cs]

    tails = [None] * CONV_CHAINS
    for cbi in range(CONV_WIDTH // LANES):
        cs = slice(cbi * LANES, (cbi + 1) * LANES)
        bias = jnp.broadcast_to(dwb_ref[:, cs], (SUBLANES, LANES))
        wts = [jnp.broadcast_to(dww_ref[j:j + 1, cs], (SUBLANES, LANES)) for j in range(CONV_K)]
        out = []
        for m0 in range(0, n_acc, CONV_CHAINS):
            width = min(CONV_CHAINS, n_acc - m0)
            accs = [bias if tails[c] is None else bias + _ordered_zero(tails[c])
                    for c in range(width)]
            loaded = {}
            for j in range(CONV_K):
                p, a = (j + shift) % n_phase, (j + shift) // n_phase
                for c in range(width):
                    key = (p, a + m0 + c)
                    if key not in loaded:
                        loaded[key] = window_vreg(p, a + m0 + c, cs)
                    accs[c] = accs[c] + loaded[key] * wts[j]
            tails[:width] = accs
            out.extend(accs)
        conv_ref[:, cs] = jnp.concatenate(out, axis=0)
    uext_ref[0:HIST, :] = uext_ref[TM:TM + HIST, :]


def _retention_tail(i, chunk_decay, proj_ref, cosp_ref, sinp_ref, coso_ref, sino_ref, mask_ref,
                    qdec_ref, kdec_ref, gng_ref, state_ref, y_ref):
    n_slabs = TM // SLAB
    per_chunk = RCHUNK // SLAB
    cb, sb = cosp_ref[0][0:1, :], sinp_ref[0][0:1, :]
    cos = cb * coso_ref[...] - sb * sino_ref[...]
    sin = sb * coso_ref[...] + cb * sino_ref[...]
    twice = lambda t: jnp.concatenate([t, t], axis=-1)

    q_bf, k_bf, qd_bf, kd_bf, v_bf = ([[None] * n_slabs for _ in range(RET_HEADS)] for _ in range(5))

    def rotary_slab(u):
        h, s = divmod(u, n_slabs)
        rows = slice(s * SLAB, (s + 1) * SLAB)
        crow = slice((s % per_chunk) * SLAB, (s % per_chunk + 1) * SLAB)
        q_rot = _rotary(proj_ref[rows, Q0 + h * HEAD_DIM:Q0 + (h + 1) * HEAD_DIM], cos[rows], sin[rows])
        k_rot = _rotary(proj_ref[rows, K0 + h * HEAD_DIM:K0 + (h + 1) * HEAD_DIM], cos[rows], sin[rows])
        q_bf[h][s], k_bf[h][s] = q_rot.astype(BF16), k_rot.astype(BF16)
        qd_bf[h][s] = (q_rot * twice(qdec_ref[h, crow, :])).astype(BF16)
        kd_bf[h][s] = (k_rot * twice(kdec_ref[h, crow, :])).astype(BF16)
        v_bf[h][s] = proj_ref[rows, V0 + h * HEAD_DIM:V0 + (h + 1) * HEAD_DIM].astype(BF16)

    yield from _slabs(RET_HEADS * n_slabs, rotary_slab)

    for h in range(RET_HEADS):
        hs = slice(h * HEAD_DIM, (h + 1) * HEAD_DIM)
        decay = jnp.where(i == 0, 1.0, chunk_decay[h])
        for c in range(TM // RCHUNK):
            chunk = lambda pieces: jnp.concatenate(pieces[h][c * per_chunk:(c + 1) * per_chunk], axis=0)
            qc, kc, qdc, kdc, vc = (chunk(p) for p in (q_bf, k_bf, qd_bf, kd_bf, v_bf))
            scores = _dot_nt(qc, kc) * mask_ref[h]
            inner = _dot(scores.astype(BF16), vc)
            state = state_ref[h]
            cross = _dot(qdc, state.astype(BF16))
            update = _dot_tn(kdc, vc)

            def state_slab(s):
                rows = slice(s * STATE_SLAB, (s + 1) * STATE_SLAB)
                state_ref[h, rows, :] = state[rows] * decay + update[rows]

            yield from _slabs(HEAD_DIM // STATE_SLAB, state_slab)
            y = inner + cross

            def norm_slab(s):
                rows = slice(s * SLAB, (s + 1) * SLAB)
                trow = slice(c * RCHUNK + s * SLAB, c * RCHUNK + (s + 1) * SLAB)
                ys = y[rows]
                mu = jnp.mean(ys, axis=-1, keepdims=True)
                var = jnp.mean(jnp.square(ys - mu), axis=-1, keepdims=True)
                yn = (ys - mu) * lax.rsqrt(var + EPS) * gng_ref[:, hs]
                gate = proj_ref[trow, G0 + h * HEAD_DIM:G0 + (h + 1) * HEAD_DIM]
                y_ref[trow, hs] = (yn * jax.nn.silu(gate)).astype(y_ref.dtype)

            yield from _slabs(per_chunk, norm_slab)


def _out_kernel(y_ref, x_ref, wout_ref, fg_ref, o_ref):
    half = TM_OUT // 2
    for r0 in (0, half):
        rows = slice(r0, r0 + half)
        h = x_ref[rows, :] + _dot(y_ref[rows, :], wout_ref[...])
        o_ref[rows, :] = _rms_norm_rows(h, fg_ref[...])


def _resident(shape):
    nd = len(shape)
    return pl.BlockSpec(shape, lambda i: (0,) * nd, pipeline_mode=pl.Buffered(1))


def kernel(x, meta_tokens, ln_g, w_in, ret_gn_g, conv_dw_w, conv_dw_b, conv_ln_g, conv_ln_b,
           conv_pw_w, conv_pw_b, w_out, final_g):
    assert x.shape == (1, SEQ, D_MODEL) and w_in.shape == (1, D_MODEL, IN_WIDTH)
    tabs, chunk_decay = _constant_tables()
    x2 = x[0]
    row = lambda a: a.reshape(1, -1)
    hbm = pl.BlockSpec(memory_space=pl.ANY)

    tile = lambda i: jnp.minimum(i, N_TILES - 1)
    prev_tile = lambda i: jnp.maximum(i - 1, 0)
    mix_in = [
        (x2, pl.BlockSpec((TM, D_MODEL), lambda i: (tile(i), 0))),
        (meta_tokens, _resident((N_META, D_MODEL))),
        (row(ln_g[0]), _resident((1, D_MODEL))),
        (w_in[0], hbm),
        (tabs["cos_b"], pl.BlockSpec((1, 8, HALF), lambda i: (prev_tile(i), 0, 0))),
        (tabs["sin_b"], pl.BlockSpec((1, 8, HALF), lambda i: (prev_tile(i), 0, 0))),
        (tabs["cos_b"], pl.BlockSpec((1, 8, HALF), lambda i: (N_TILES, 0, 0))),
        (tabs["sin_b"], pl.BlockSpec((1, 8, HALF), lambda i: (N_TILES, 0, 0))),
        (tabs["cos_o"], _resident((TM, HALF))),
        (tabs["sin_o"], _resident((TM, HALF))),
        (tabs["mask"], _resident((RET_HEADS, RCHUNK, RCHUNK))),
        (tabs["qdec"], _resident((RET_HEADS, RCHUNK, LANES))),
        (tabs["kdec"], _resident((RET_HEADS, RCHUNK, LANES))),
        (row(ret_gn_g[0]), _resident((1, RET_WIDTH))),
        (conv_dw_w[0], _resident((CONV_K, CONV_WIDTH))),
        (row(conv_dw_b[0]), _resident((1, CONV_WIDTH))),
        (row(conv_ln_g[0]), _resident((1, CONV_WIDTH))),
        (row(conv_ln_b[0]), _resident((1, CONV_WIDTH))),
        (conv_pw_w[0], hbm),
        (row(conv_pw_b[0]), _resident((1, CONV_WIDTH))),
        (w_out[0], pl.BlockSpec((WOUT_ROWS, D_MODEL), lambda i: (tile(i), 0))),
    ]
    y, w_out_bf = pl.pallas_call(
        functools.partial(_mix_kernel, chunk_decay),
        grid=(N_TILES + 1,),
        in_specs=[s for _, s in mix_in],
        out_specs=[pl.BlockSpec((TM, D_MODEL), lambda i: (prev_tile(i), 0)),
                   pl.BlockSpec((WOUT_ROWS, D_MODEL), lambda i: (tile(i), 0))],
        out_shape=[jax.ShapeDtypeStruct((SEQ, D_MODEL), BF16),
                   jax.ShapeDtypeStruct((D_MODEL, D_MODEL), BF16)],
        scratch_shapes=[
            pltpu.VMEM((D_MODEL, IN_WIDTH), BF16),
            pltpu.VMEM((CONV_WIDTH, CONV_WIDTH), BF16),
            pltpu.SemaphoreType.DMA((TM // STAGE_ROWS,)),
            pltpu.VMEM((RET_HEADS, HEAD_DIM, HEAD_DIM), F32),
            pltpu.VMEM((HIST + TM, CONV_WIDTH), F32),
            pltpu.VMEM((SUBLANES - 1, SUBLANES + HIST + TM, CONV_WIDTH), F32),
            pltpu.VMEM((TM, PROJ_GC + CONV_WIDTH), F32),
            pltpu.VMEM((TM, CONV_WIDTH), F32),
            pltpu.VMEM((TM, D_MODEL), BF16),
            pltpu.VMEM((TM, CONV_WIDTH), BF16),
        ],
        compiler_params=pltpu.CompilerParams(
            dimension_semantics=("arbitrary",), vmem_limit_bytes=VMEM_LIMIT_MIX),
        name="mix",
    )(*[a for a, _ in mix_in])

    out = pl.pallas_call(
        _out_kernel,
        grid=(SEQ // TM_OUT,),
        in_specs=[
            pl.BlockSpec((TM_OUT, D_MODEL), lambda i: (i, 0)),
            pl.BlockSpec((TM_OUT, D_MODEL), lambda i: (i, 0)),
            _resident((D_MODEL, D_MODEL)),
            _resident((1, D_MODEL)),
        ],
        out_specs=pl.BlockSpec((TM_OUT, D_MODEL), lambda i: (i, 0)),
        out_shape=jax.ShapeDtypeStruct((SEQ, D_MODEL), x.dtype),
        compiler_params=pltpu.CompilerParams(
            dimension_semantics=("arbitrary",), vmem_limit_bytes=VMEM_LIMIT_OUT),
        name="out",
    )(y, x2, w_out_bf, row(final_g))
    return out[None]
```
